```python
import jax
import jax.numpy as jnp
from jax import lax
import numpy as np

D_MODEL = 1024
BATCH = 2
SEQ = 16384
DEPTH = 2
DEC_BATCH = 1
DEC_SEQ = 16384
PAST_LEN = 128

F32 = jnp.float32
NORM_EPS = 1e-6

N_GROUPS = 4
GROUP_W = D_MODEL // N_GROUPS
HEAD_DIM = 64
GROUP_HEADS = GROUP_W // HEAD_DIM
D_MIX = N_GROUPS * GROUP_W

MLSTM_CHUNK = 128
MLSTM_IN = 4 * GROUP_W + 4 * GROUP_HEADS

MLA_Q_LORA = 192
MLA_KV_LORA = 128
MLA_NOPE = 64
MLA_ROPE = 32
MLA_QK = MLA_NOPE + MLA_ROPE
MLA_V = HEAD_DIM
MLA_IN = MLA_Q_LORA + MLA_KV_LORA + MLA_ROPE
Q_BLOCK = 128
ROPE_THETA = 10000.0

RWKV_DECAY_LORA = 32
RWKV_A_LORA = 32
RWKV_GATE_LORA = 64
RWKV_IN = 3 * GROUP_W + RWKV_DECAY_LORA + RWKV_A_LORA + RWKV_GATE_LORA
RWKV_GN_EPS = 64e-5

LRU_CONV = 4
LRU_C = 8.0
LRU_IN = 2 * GROUP_W

N_IN = MLSTM_IN + MLA_IN + RWKV_IN + LRU_IN

N_MEM = 256
XA_HEADS = 4
XA_DIM = 64

N_EXPERTS = 16
EXPERT_FF = 1024
CAPACITY_FACTOR = 2

kernel_name = 'hybrid_bidir_encoder_ec_moe'


def rmsnorm(x, g, eps=NORM_EPS):
    xf = x.astype(F32)
    y = xf * lax.rsqrt(jnp.mean(xf * xf, axis=-1, keepdims=True) + eps)
    return (y * g.astype(F32)).astype(x.dtype)


def rope_tables(T):
    inv = 1.0 / (ROPE_THETA ** (jnp.arange(0, MLA_ROPE, 2, dtype=F32) / MLA_ROPE))
    ang = jnp.arange(T, dtype=F32)[:, None] * inv[None, :]
    return jnp.cos(ang), jnp.sin(ang)


def apply_rope(x, cos, sin):
    xf = x.astype(F32)
    x1, x2 = jnp.split(xf, 2, axis=-1)
    c = cos[None, :, None, :]
    s = sin[None, :, None, :]
    return jnp.concatenate([x1 * c - x2 * s, x1 * s + x2 * c], axis=-1).astype(x.dtype)


def blocked_attention(q, k, v):
    B, H, T, Dk = q.shape
    nb = T // Q_BLOCK
    scale = Dk ** -0.5
    qb = jnp.moveaxis(q.reshape(B, H, nb, Q_BLOCK, Dk), 2, 0)

    def one_block(qblk):
        s = jnp.einsum('bhqd,bhkd->bhqk', qblk, k, preferred_element_type=F32) * scale
        p = jax.nn.softmax(s, axis=-1).astype(v.dtype)
        return jnp.einsum('bhqk,bhkd->bhqd', p, v)

    o = lax.map(one_block, qb)
    return jnp.moveaxis(o, 0, 2).reshape(B, H, T, v.shape[-1])


def mlstm_chunkwise(q, k, v, log_i, log_f):
    B, H, T, Dh = q.shape
    nc = T // MLSTM_CHUNK

    def chunks(t):
        return jnp.moveaxis(t.reshape((B, H, nc, MLSTM_CHUNK) + t.shape[3:]), 2, 0)

    xs = (chunks(q), chunks(k), chunks(v), chunks(log_i), chunks(log_f))
    lower = jnp.tril(jnp.ones((MLSTM_CHUNK, MLSTM_CHUNK), dtype=bool))

    def step(carry, inp):
        c_st, n_st, m_st = carry
        qj, kj, vj, ij, fj = inp
        b = jnp.cumsum(fj, axis=-1)
        dmat = jnp.where(lower, b[..., :, None] - b[..., None, :] + ij[..., None, :], -jnp.inf)
        inter = b + m_st[..., None]
        m_row = jnp.maximum(inter, jnp.max(dmat, axis=-1))
        s = jnp.einsum('bhld,bhsd->bhls', qj, kj) * jnp.exp(dmat - m_row[..., None])
        w_inter = jnp.exp(inter - m_row)
        num = jnp.einsum('bhls,bhsd->bhld', s, vj) + w_inter[..., None] * jnp.einsum('bhvk,bhlk->bhlv', c_st, qj)
        den = jnp.sum(s, axis=-1) + w_inter * jnp.einsum('bhk,bhlk->bhl', n_st, qj)
        h = num / jnp.maximum(jnp.abs(den), jnp.exp(-m_row))[..., None]
        b_end = b[..., -1]
        g = b_end[..., None] - b + ij
        m_new = jnp.maximum(b_end + m_st, jnp.max(g, axis=-1))
        wg = jnp.exp(g - m_new[..., None])
        carry_scale = jnp.exp(b_end + m_st - m_new)
        c_new = carry_scale[..., None, None] * c_st + jnp.einsum('bhs,bhsv,bhsk->bhvk', wg, vj, kj)
        n_new = carry_scale[..., None] * n_st + jnp.einsum('bhs,bhsk->bhk', wg, kj)
        return (c_new, n_new, m_new), h

    init = (jnp.zeros((B, H, Dh, Dh), F32), jnp.zeros((B, H, Dh), F32), jnp.zeros((B, H), F32))
    _, hs = lax.scan(step, init, xs)
    return jnp.moveaxis(hs, 0, 2).reshape(B, H, T, Dh)


def mlstm_mixer(p_m, gate_b, head_g):
    B, T, _ = p_m.shape
    pf = p_m.astype(F32)
    q, k, v, o, gates = jnp.split(pf, [GROUP_W, 2 * GROUP_W, 3 * GROUP_W, 4 * GROUP_W], axis=-1)

    def heads(t):
        return jnp.moveaxis(t.reshape(B, T, GROUP_HEADS, HEAD_DIM), 2, 1)

    q, k, v = heads(q), heads(k) * HEAD_DIM ** -0.5, heads(v)
    g = jnp.transpose((gates + gate_b.astype(F32)).reshape(B, T, 4, GROUP_HEADS), (2, 0, 3, 1))
    i_fw, f_fw, i_bw, f_bw = g[0], g[1], g[2], g[3]
    h_fw = mlstm_chunkwise(q, k, v, i_fw, jax.nn.log_sigmoid(f_fw))

    def flip(t):
        return jnp.flip(t, axis=2)

    h_bw = flip(mlstm_chunkwise(flip(q), flip(k), flip(v), flip(i_bw), flip(jax.nn.log_sigmoid(f_bw))))
    h = rmsnorm(jnp.moveaxis(h_fw + h_bw, 1, 2), head_g.reshape(GROUP_HEADS, HEAD_DIM))
    return h.reshape(B, T, GROUP_W) * jax.nn.sigmoid(o)


def mla_mixer(p_a, qa_g, w_uq, kva_g, w_ukv, q_g, k_g, cos, sin):
    B, T, _ = p_a.shape
    c_q, c_kv, k_pe = jnp.split(p_a, [MLA_Q_LORA, MLA_Q_LORA + MLA_KV_LORA], axis=-1)
    q = (rmsnorm(c_q, qa_g) @ w_uq).reshape(B, T, GROUP_HEADS, MLA_QK)
    kv = (rmsnorm(c_kv, kva_g) @ w_ukv).reshape(B, T, GROUP_HEADS, MLA_NOPE + MLA_V)
    k_nope, v = jnp.split(kv, [MLA_NOPE], axis=-1)
    k = jnp.concatenate([k_nope, jnp.broadcast_to(k_pe[:, :, None, :], (B, T, GROUP_HEADS, MLA_ROPE))], axis=-1)
    q = rmsnorm(q, q_g)
    k = rmsnorm(k, k_g)
    q = jnp.concatenate([q[..., :MLA_NOPE], apply_rope(q[..., MLA_NOPE:], cos, sin)], axis=-1)
    k = jnp.concatenate([k[..., :MLA_NOPE], apply_rope(k[..., MLA_NOPE:], cos, sin)], axis=-1)
    o = blocked_attention(jnp.moveaxis(q, 2, 1), jnp.moveaxis(k, 2, 1), jnp.moveaxis(v, 2, 1))
    return jnp.moveaxis(o, 1, 2).reshape(B, T, GROUP_HEADS * MLA_V)


def rwkv7_scan(r, w, k, v, a, b, reverse):
    B, T, H, N = r.shape

    def step(S, inp):
        r_t, w_t, k_t, v_t, a_t, b_t = inp
        sa = jnp.einsum('bhij,bhj->bhi', S, a_t)
        S = S * w_t[:, :, None, :] + sa[..., None] * b_t[:, :, None, :] + v_t[..., None] * k_t[:, :, None, :]
        return S, jnp.einsum('bhij,bhj->bhi', S, r_t)

    xs = tuple(jnp.moveaxis(t, 1, 0) for t in (r, w, k, v, a, b))
    _, ys = lax.scan(step, jnp.zeros((B, H, N, N), F32), xs, reverse=reverse)
    return jnp.moveaxis(ys, 0, 1)


def rwkv7_mixer(p_r, mu, w0, w_w2, a0, w_a2, w_g2, k_k, k_a, r_k, ln_g, ln_b):
    B, T, _ = p_r.shape
    pf = p_r.astype(F32)
    prev = jnp.pad(pf, ((0, 0), (1, 0), (0, 0)))[:, :-1]
    nxt = jnp.pad(pf, ((0, 0), (0, 1), (0, 0)))[:, 1:]
    pf = pf + mu[0] * (prev - pf) + mu[1] * (nxt - pf)
    o1 = 3 * GROUP_W
    o2 = o1 + RWKV_DECAY_LORA
    o3 = o2 + RWKV_A_LORA
    r, k, v, wd, ad, gd = jnp.split(pf, [GROUP_W, 2 * GROUP_W, o1, o2, o3], axis=-1)

    def hs(t):
        return t.reshape(B, T, GROUP_HEADS, HEAD_DIM)

    kk = hs(k * k_k)
    kk = kk * lax.rsqrt(jnp.sum(kk * kk, axis=-1, keepdims=True) + 1e-12)
    wd_t = jnp.tanh(wd)
    ys = []
    k_mod = []
    for d in range(2):
        w_log = -jax.nn.softplus(-(w0[d] + wd_t @ w_w2[d])) - 0.5
        decay = jnp.exp(-jnp.exp(w_log))
        a = jax.nn.sigmoid(a0[d] + ad @ w_a2[d])
        k_d = k * (1.0 + (a - 1.0) * k_a)
        ys.append(rwkv7_scan(hs(r), hs(decay), hs(k_d), hs(v), -kk, kk * hs(a), reverse=(d == 1)))
        k_mod.append(k_d)
    y = ys[0] + ys[1]
    mean = jnp.mean(y, axis=-1, keepdims=True)
    var = jnp.mean(jnp.square(y - mean), axis=-1, keepdims=True)
    y = (y - mean) * lax.rsqrt(var + RWKV_GN_EPS) * ln_g.reshape(GROUP_HEADS, HEAD_DIM) + ln_b.reshape(GROUP_HEADS, HEAD_DIM)
    bonus = jnp.sum(hs(r) * hs(k_mod[0] + k_mod[1]) * r_k.reshape(GROUP_HEADS, HEAD_DIM), axis=-1, keepdims=True) * hs(v)
    gate = jax.nn.sigmoid(gd) @ w_g2
    return (y + bonus).reshape(B, T, GROUP_W) * gate


def linear_scan(a, u, reverse):
    def combine(e1, e2):
        a1, b1 = e1
        a2, b2 = e2
        return a1 * a2, a2 * b1 + b2
    return lax.associative_scan(combine, (a, u), reverse=reverse, axis=1)[1]


def rglru_mixer(p_l, conv_w, conv_b, gate_w, gate_b, lam):
    B, T, _ = p_l.shape
    xb, gb = jnp.split(p_l.astype(F32), 2, axis=-1)
    left = LRU_CONV // 2
    xp = jnp.pad(xb, ((0, 0), (left, LRU_CONV - 1 - left), (0, 0)))
    xc = conv_b.astype(F32)
    for j in range(LRU_CONV):
        xc = xc + xp[:, j:j + T] * conv_w[j]
    xblk = xc.reshape(B, T, GROUP_HEADS, HEAD_DIM)
    hs = []
    for d in range(2):
        gts = jnp.einsum('btnd,gnde->gbtne', xblk, gate_w[d]).reshape(2, B, T, GROUP_W) + gate_b[d][:, None, None, :]
        r_gate = jax.nn.sigmoid(gts[0])
        i_gate = jax.nn.sigmoid(gts[1])
        log_a = -LRU_C * r_gate * jax.nn.softplus(-lam[d])
        u = jnp.sqrt(-jnp.expm1(2.0 * log_a)) * (i_gate * xc)
        hs.append(linear_scan(jnp.exp(log_a), u, reverse=(d == 1)))
    return (hs[0] + hs[1]) * jax.nn.gelu(gb)


def memory_cross_attention(h, m, wq, wkv, q_g, k_g, wo):
    B, T, _ = h.shape
    M = m.shape[1]
    q = rmsnorm((h @ wq).reshape(B, T, XA_HEADS, XA_DIM), q_g)
    kv = (m @ wkv).reshape(B, M, 2, XA_HEADS, XA_DIM)
    k = rmsnorm(kv[:, :, 0], k_g)
    v = kv[:, :, 1]
    s = jnp.einsum('bqhd,bkhd->bhqk', q, k, preferred_element_type=F32) * XA_DIM ** -0.5
    p = jax.nn.softmax(s, axis=-1).astype(v.dtype)
    o = jnp.einsum('bhqk,bkhd->bqhd', p, v).reshape(B, T, XA_HEADS * XA_DIM)
    return o @ wo


def expert_choice_ffn(h, router, w_gate, w_up, w_down):
    B, T, D = h.shape
    n_tok = B * T
    cap = max(1, (CAPACITY_FACTOR * n_tok) // N_EXPERTS)
    tok = h.reshape(n_tok, D)
    aff = jax.nn.softmax(jnp.einsum('nd,de->ne', tok, router, preferred_element_type=F32), axis=-1)
    gate, idx = lax.top_k(aff.T, cap)
    xe = jnp.take(tok, idx, axis=0)
    hid = jax.nn.silu(jnp.einsum('ecd,edf->ecf', xe, w_gate)) * jnp.einsum('ecd,edf->ecf', xe, w_up)
    ye = jnp.einsum('ecf,efd->ecd', hid, w_down) * gate[..., None].astype(h.dtype)
    out = jnp.zeros_like(tok).at[idx.reshape(-1)].add(ye.reshape(-1, D))
    return out.reshape(B, T, D)


def encoder_layer(x, mem, p, cos, sin):
    h = rmsnorm(x, p['norm_mix_g'])
    proj = h @ p['w_in']
    s1 = MLSTM_IN
    s2 = s1 + MLA_IN
    s3 = s2 + RWKV_IN
    p_m, p_a, p_r, p_l = jnp.split(proj, [s1, s2, s3], axis=-1)
    y_m = mlstm_mixer(p_m, p['mlstm_gate_b'], p['mlstm_head_g'])
    y_a = mla_mixer(p_a, p['mla_qa_g'], p['mla_w_uq'], p['mla_kva_g'], p['mla_w_ukv'], p['mla_q_g'], p['mla_k_g'], cos, sin)
    y_r = rwkv7_mixer(p_r, p['rwkv_mu'], p['rwkv_w0'], p['rwkv_w_w2'], p['rwkv_a0'], p['rwkv_w_a2'], p['rwkv_w_g2'],
                      p['rwkv_k_k'], p['rwkv_k_a'], p['rwkv_r_k'], p['rwkv_ln_g'], p['rwkv_ln_b'])
    y_l = rglru_mixer(p_l, p['lru_conv_w'], p['lru_conv_b'], p['lru_gate_w'], p['lru_gate_b'], p['lru_lambda'])
    y = jnp.concatenate([y_m.astype(x.dtype), y_a.astype(x.dtype), y_r.astype(x.dtype), y_l.astype(x.dtype)], axis=-1)
    x = x + y @ p['w_out']
    x = x + memory_cross_attention(rmsnorm(x, p['norm_xa_g']), rmsnorm(mem, p['norm_mem_g']),
                                   p['xa_wq'], p['xa_wkv'], p['xa_q_g'], p['xa_k_g'], p['xa_wo'])
    x = x + expert_choice_ffn(rmsnorm(x, p['norm_ffn_g']), p['moe_router'], p['moe_w_gate'], p['moe_w_up'], p['moe_w_down'])
    return x


def setup_inputs(seed: int = 0) -> dict:
    key = jax.random.key(seed)
    ks = jax.random.split(key, 64)
    counter = iter(range(64))

    def nxt():
        return ks[next(counter)]

    def nrm(shape, scale):
        return scale * jax.random.normal(nxt(), shape, F32)

    def gain(shape):
        return 1.0 + 0.05 * jax.random.normal(nxt(), shape, F32)

    def unif(shape, lo, hi):
        return jax.random.uniform(nxt(), shape, F32, lo, hi)

    L = DEPTH
    x_prompt = jax.random.normal(nxt(), (BATCH, SEQ, D_MODEL), F32)
    x_sample = jax.random.normal(nxt(), (DEC_BATCH, DEC_SEQ, D_MODEL), F32)
    mem_prompt = jax.random.normal(nxt(), (BATCH, N_MEM, D_MODEL), F32)
    mem_sample = jax.random.normal(nxt(), (DEC_BATCH, N_MEM, D_MODEL), F32)

    i_fw = nrm((L, GROUP_HEADS), 0.1)
    f_fw = jnp.linspace(3.0, 6.0, GROUP_HEADS, dtype=F32) + nrm((L, GROUP_HEADS), 0.1)
    i_bw = nrm((L, GROUP_HEADS), 0.1)
    f_bw = jnp.linspace(3.0, 6.0, GROUP_HEADS, dtype=F32) + nrm((L, GROUP_HEADS), 0.1)
    mlstm_gate_b = jnp.concatenate([i_fw, f_fw, i_bw, f_bw], axis=-1)

    a_init = unif((L, 2, GROUP_W), 0.9, 0.999) ** (1.0 / LRU_C)
    lru_lambda = jnp.log(a_init) - jnp.log1p(-a_init)

    return {
        'x_prompt': x_prompt,
        'x_sample': x_sample,
        'mem_prompt': mem_prompt,
        'mem_sample': mem_sample,
        'norm_mix_g': gain((L, D_MODEL)),
        'w_in': nrm((L, D_MODEL, N_IN), D_MODEL ** -0.5),
        'mlstm_gate_b': mlstm_gate_b,
        'mlstm_head_g': gain((L, GROUP_W)),
        'mla_qa_g': gain((L, MLA_Q_LORA)),
        'mla_w_uq': nrm((L, MLA_Q_LORA, GROUP_HEADS * MLA_QK), MLA_Q_LORA ** -0.5),
        'mla_kva_g': gain((L, MLA_KV_LORA)),
        'mla_w_ukv': nrm((L, MLA_KV_LORA, GROUP_HEADS * (MLA_NOPE + MLA_V)), MLA_KV_LORA ** -0.5),
        'mla_q_g': gain((L, MLA_QK)),
        'mla_k_g': gain((L, MLA_QK)),
        'rwkv_mu': unif((L, 2, RWKV_IN), 0.0, 0.5),
        'rwkv_w0': unif((L, 2, GROUP_W), -6.0, 1.0),
        'rwkv_w_w2': nrm((L, 2, RWKV_DECAY_LORA, GROUP_W), 0.1),
        'rwkv_a0': nrm((L, 2, GROUP_W), 0.5),
        'rwkv_w_a2': nrm((L, 2, RWKV_A_LORA, GROUP_W), 0.1),
        'rwkv_w_g2': nrm((L, RWKV_GATE_LORA, GROUP_W), RWKV_GATE_LORA ** -0.5),
        'rwkv_k_k': 0.85 + 0.05 * jax.random.normal(nxt(), (L, GROUP_W), F32),
        'rwkv_k_a': gain((L, GROUP_W)),
        'rwkv_r_k': nrm((L, GROUP_W), 0.1),
        'rwkv_ln_g': gain((L, GROUP_W)),
        'rwkv_ln_b': nrm((L, GROUP_W), 0.02),
        'lru_conv_w': nrm((L, LRU_CONV, GROUP_W), LRU_CONV ** -0.5),
        'lru_conv_b': nrm((L, GROUP_W), 0.02),
        'lru_gate_w': nrm((L, 2, 2, GROUP_HEADS, HEAD_DIM, HEAD_DIM), HEAD_DIM ** -0.5),
        'lru_gate_b': nrm((L, 2, 2, GROUP_W), 0.02),
        'lru_lambda': lru_lambda,
        'w_out': nrm((L, D_MIX, D_MODEL), D_MIX ** -0.5),
        'norm_xa_g': gain((L, D_MODEL)),
        'norm_mem_g': gain((L, D_MODEL)),
        'xa_wq': nrm((L, D_MODEL, XA_HEADS * XA_DIM), D_MODEL ** -0.5),
        'xa_wkv': nrm((L, D_MODEL, 2 * XA_HEADS * XA_DIM), D_MODEL ** -0.5),
        'xa_q_g': gain((L, XA_DIM)),
        'xa_k_g': gain((L, XA_DIM)),
        'xa_wo': nrm((L, XA_HEADS * XA_DIM, D_MODEL), (XA_HEADS * XA_DIM) ** -0.5),
        'norm_ffn_g': gain((L, D_MODEL)),
        'moe_router': nrm((L, D_MODEL, N_EXPERTS), D_MODEL ** -0.5),
        'moe_w_gate': nrm((L, N_EXPERTS, D_MODEL, EXPERT_FF), D_MODEL ** -0.5),
        'moe_w_up': nrm((L, N_EXPERTS, D_MODEL, EXPERT_FF), D_MODEL ** -0.5),
        'moe_w_down': nrm((L, N_EXPERTS, EXPERT_FF, D_MODEL), EXPERT_FF ** -0.5),
    }


def reference(x_prompt, x_sample, mem_prompt, mem_sample, norm_mix_g, w_in, mlstm_gate_b, mlstm_head_g,
              mla_qa_g, mla_w_uq, mla_kva_g, mla_w_ukv, mla_q_g, mla_k_g,
              rwkv_mu, rwkv_w0, rwkv_w_w2, rwkv_a0, rwkv_w_a2, rwkv_w_g2, rwkv_k_k, rwkv_k_a, rwkv_r_k,
              rwkv_ln_g, rwkv_ln_b, lru_conv_w, lru_conv_b, lru_gate_w, lru_gate_b, lru_lambda, w_out,
              norm_xa_g, norm_mem_g, xa_wq, xa_wkv, xa_q_g, xa_k_g, xa_wo,
              norm_ffn_g, moe_router, moe_w_gate, moe_w_up, moe_w_down):
    params = dict(norm_mix_g=norm_mix_g, w_in=w_in, mlstm_gate_b=mlstm_gate_b, mlstm_head_g=mlstm_head_g,
                  mla_qa_g=mla_qa_g, mla_w_uq=mla_w_uq, mla_kva_g=mla_kva_g, mla_w_ukv=mla_w_ukv,
                  mla_q_g=mla_q_g, mla_k_g=mla_k_g,
                  rwkv_mu=rwkv_mu, rwkv_w0=rwkv_w0, rwkv_w_w2=rwkv_w_w2, rwkv_a0=rwkv_a0, rwkv_w_a2=rwkv_w_a2,
                  rwkv_w_g2=rwkv_w_g2, rwkv_k_k=rwkv_k_k, rwkv_k_a=rwkv_k_a, rwkv_r_k=rwkv_r_k,
                  rwkv_ln_g=rwkv_ln_g, rwkv_ln_b=rwkv_ln_b,
                  lru_conv_w=lru_conv_w, lru_conv_b=lru_conv_b, lru_gate_w=lru_gate_w, lru_gate_b=lru_gate_b,
                  lru_lambda=lru_lambda, w_out=w_out,
                  norm_xa_g=norm_xa_g, norm_mem_g=norm_mem_g, xa_wq=xa_wq, xa_wkv=xa_wkv, xa_q_g=xa_q_g,
                  xa_k_g=xa_k_g, xa_wo=xa_wo,
                  norm_ffn_g=norm_ffn_g, moe_router=moe_router, moe_w_gate=moe_w_gate, moe_w_up=moe_w_up,
                  moe_w_down=moe_w_down)

    def trunk(x, mem):
        cos, sin = rope_tables(x.shape[1])
        for l in range(DEPTH):
            p_l = {name: arr[l] for name, arr in params.items()}
            x = encoder_layer(x, mem, p_l, cos, sin)
        return x

    y_prompt = trunk(x_prompt, mem_prompt)
    y_sample = trunk(x_sample, mem_sample)
    return (y_prompt, y_sample)
```

```python
import functools
import math

import jax
import jax.numpy as jnp
from jax import lax
from jax.experimental import pallas as pl
from jax.experimental.pallas import tpu as pltpu

F32 = jnp.float32
BF16 = jnp.bfloat16
HI = lax.Precision.HIGHEST

D_MODEL = 1024
NORM_EPS = 1e-6
GROUP_W = 256
HEAD_DIM = 64
GROUP_HEADS = 4

MLSTM_CHUNK = 128
MLSTM_IN = 4 * GROUP_W + 4 * GROUP_HEADS
MLSTM_PAD = 1152

MLA_Q_LORA = 192
MLA_KV_LORA = 128
MLA_NOPE = 64
MLA_ROPE = 32
MLA_QK = MLA_NOPE + MLA_ROPE
MLA_IN = MLA_Q_LORA + MLA_KV_LORA + MLA_ROPE
MLA_PAD = 512
ROPE_THETA = 10000.0

RWKV_IN = 3 * GROUP_W + 32 + 32 + 64
RWKV_GN_EPS = 64e-5
RWKV_CHUNK = 64

LRU_CONV = 4
LRU_C = 8.0
LRU_IN = 2 * GROUP_W

N_EXPERTS = 16
CAPACITY_FACTOR = 2
XA_HEADS = 4
XA_DIM = 64

VMEM_LIMIT_BYTES = 56 * 1024 * 1024
SUBLANES = 8
LANES = 128


def _cparams(sem):
    return pltpu.CompilerParams(dimension_semantics=sem, vmem_limit_bytes=VMEM_LIMIT_BYTES)


def _dot(a, b, prec=None):
    return jnp.dot(a, b, preferred_element_type=F32, precision=prec)


def _dot_nt(a, b, prec=None):
    return lax.dot_general(a, b, (((1,), (1,)), ((), ())), preferred_element_type=F32, precision=prec)


def _dot_tn(a, b, prec=None):
    return lax.dot_general(a, b, (((0,), (0,)), ((), ())), preferred_element_type=F32, precision=prec)


def _sigmoid(x):
    return 1.0 / (1.0 + jnp.exp(-x))


def _log_sigmoid(x):
    return jnp.minimum(x, 0.0) - jnp.log(1.0 + jnp.exp(-jnp.abs(x)))


def _rms(x, n):
    return x * lax.rsqrt(jnp.sum(x * x, axis=-1, keepdims=True) * (1.0 / n) + NORM_EPS)


def _head_id(shape, axis, hd=HEAD_DIM):
    return lax.broadcasted_iota(jnp.int32, shape, axis) // hd


def _head_sum_bcast(x, hid, nheads=GROUP_HEADS):
    out = jnp.zeros_like(x)
    for h in range(nheads):
        hm = hid == h
        s = jnp.sum(jnp.where(hm, x, 0.0), axis=-1, keepdims=True)
        out = jnp.where(hm, s, out)
    return out


IN_SPLITS = (MLSTM_PAD, MLA_PAD, RWKV_IN, LRU_IN)
IN_TOTAL = sum(IN_SPLITS)


def _in_proj_kernel(x_ref, g_ref, w_ref, pm_ref, pa_ref, pr_ref, pl_ref):
    x = x_ref[...]
    h = (_rms(x, D_MODEL) * g_ref[...]).astype(BF16)
    off = 0
    for ref, width in zip((pm_ref, pa_ref, pr_ref, pl_ref), IN_SPLITS):
        ref[...] = _dot(h, w_ref[:, off:off + width])
        off += width


def in_proj(x, g, w):
    n = x.shape[0]
    tm = min(256, n)
    return pl.pallas_call(
        _in_proj_kernel,
        grid=(n // tm,),
        in_specs=[
            pl.BlockSpec((tm, D_MODEL), lambda i: (i, 0)),
            pl.BlockSpec((1, D_MODEL), lambda i: (0, 0)),
            pl.BlockSpec((D_MODEL, IN_TOTAL), lambda i: (0, 0)),
        ],
        out_specs=[pl.BlockSpec((tm, wd), lambda i: (i, 0)) for wd in IN_SPLITS],
        out_shape=[jax.ShapeDtypeStruct((n, wd), F32) for wd in IN_SPLITS],
        compiler_params=_cparams(("parallel",)),
        name="in_proj",
    )(x, g, w)


def prep_w_in(w_in):
    s1 = MLSTM_IN
    s2 = s1 + MLA_IN
    s3 = s2 + RWKV_IN
    nl = w_in.shape[0]

    def z(width):
        return jnp.zeros((nl, D_MODEL, width), F32)

    w_m = jnp.concatenate([w_in[:, :, :s1], z(MLSTM_PAD - MLSTM_IN)], axis=-1)
    a = w_in[:, :, s1:s2]
    w_a = jnp.concatenate([
        a[:, :, :MLA_Q_LORA], z(256 - MLA_Q_LORA),
        a[:, :, MLA_Q_LORA:MLA_Q_LORA + MLA_KV_LORA],
        z(MLA_NOPE), a[:, :, MLA_Q_LORA + MLA_KV_LORA:], z(128 - MLA_NOPE - MLA_ROPE),
    ], axis=-1)
    return jnp.concatenate([w_m, w_a, w_in[:, :, s2:s3], w_in[:, :, s3:]], axis=-1).astype(BF16)


def _mlstm_kernel(rev, *refs):
    if rev:
        pm_ref, bias_ref, hfw_ref, hg_ref, out_ref, ct_scr, n_scr, m_scr = refs
    else:
        pm_ref, bias_ref, out_ref, ct_scr, n_scr, m_scr = refs
    L = MLSTM_CHUNK

    @pl.when(pl.program_id(1) == 0)
    def _():
        ct_scr[...] = jnp.zeros_like(ct_scr)
        n_scr[...] = jnp.zeros_like(n_scr)
        m_scr[...] = jnp.zeros_like(m_scr)

    p = pm_ref[0]
    q = p[:, 0:GROUP_W]
    k = p[:, GROUP_W:2 * GROUP_W] * (HEAD_DIM ** -0.5)
    v = p[:, 2 * GROUP_W:3 * GROUP_W]
    gt = p[:, 4 * GROUP_W:4 * GROUP_W + LANES] + bias_ref[...]
    ioff, foff = (2 * GROUP_HEADS, 3 * GROUP_HEADS) if rev else (0, GROUP_HEADS)
    logf = _log_sigmoid(gt)
    row = lax.broadcasted_iota(jnp.int32, (L, L), 0)
    col = lax.broadcasted_iota(jnp.int32, (L, L), 1)
    tri = (col >= row) if rev else (col <= row)
    bcum = _dot(tri.astype(F32), logf, HI)
    bcum_t = bcum.T
    gt_t = gt.T
    hid = _head_id((L, GROUP_W), 1)
    hid_row = hid[0:1, :]
    rid = _head_id((GROUP_W, GROUP_W), 0)
    cid = _head_id((GROUP_W, GROUP_W), 1)

    qb = q.astype(BF16)
    kb = k.astype(BF16)
    vb = v.astype(BF16)
    n_row = n_scr[0:1, :]
    qc = _dot(qb, ct_scr[...].astype(BF16))
    qn = _head_sum_bcast(q * n_row, hid)

    acc = jnp.zeros((L, GROUP_W), F32)
    wg_full = jnp.zeros((L, GROUP_W), F32)
    cs_row = jnp.zeros((1, GROUP_W), F32)
    cs_full = jnp.zeros((GROUP_W, GROUP_W), F32)
    for h in range(GROUP_HEADS):
        hm = hid == h
        i_row = gt_t[ioff + h:ioff + h + 1, :]
        i_col = gt[:, ioff + h:ioff + h + 1]
        b_row = bcum_t[foff + h:foff + h + 1, :]
        b_col = bcum[:, foff + h:foff + h + 1]
        m_st = m_scr[h:h + 1, 0:1]
        dmat = jnp.where(tri, b_col - b_row + i_row, -jnp.inf)
        inter = b_col + m_st
        m_row = jnp.maximum(inter, jnp.max(dmat, axis=-1, keepdims=True))
        s = _dot_nt(jnp.where(hm, q, 0.0).astype(BF16), kb) * jnp.exp(dmat - m_row)
        w_inter = jnp.exp(inter - m_row)
        num = _dot(s.astype(BF16), vb) + w_inter * qc
        den = jnp.sum(s, axis=-1, keepdims=True) + w_inter * qn
        hh = num / jnp.maximum(jnp.abs(den), jnp.exp(-m_row))
        acc = jnp.where(hm, hh, acc)

        b_end = b_col[0:1, :] if rev else b_col[L - 1:L, :]
        g = b_end - b_col + i_col
        m_new = jnp.maximum(b_end + m_st, jnp.max(g, axis=0, keepdims=True))
        wg_full = jnp.where(hm, jnp.exp(g - m_new), wg_full)
        cs = jnp.exp(b_end + m_st - m_new)
        cs_row = jnp.where(hid_row == h, cs, cs_row)
        cs_full = jnp.where(rid == h, cs, cs_full)
        m_scr[h:h + 1, :] = jnp.broadcast_to(m_new, (1, LANES))

    kw = k * wg_full
    ct_new = cs_full * ct_scr[...] + _dot_tn(kw.astype(BF16), vb)
    ct_scr[...] = jnp.where(rid == cid, ct_new, 0.0)
    n_scr[0:1, :] = cs_row * n_row + jnp.sum(kw, axis=0, keepdims=True)

    if rev:
        hs = hfw_ref[0] + acc
        ms = _head_sum_bcast(hs * hs, hid) * (1.0 / HEAD_DIM)
        y = hs * lax.rsqrt(ms + NORM_EPS) * hg_ref[...]
        o = p[:, 3 * GROUP_W:4 * GROUP_W]
        out_ref[0] = (y * _sigmoid(o)).astype(out_ref.dtype)
    else:
        out_ref[0] = acc


def mlstm_mixer(pm, gate_b, head_g):
    S, T, _ = pm.shape
    L = MLSTM_CHUNK
    nc = T // L
    bias = jnp.zeros((1, LANES), F32).at[0, :4 * GROUP_HEADS].set(gate_b)
    scratch = [pltpu.VMEM((GROUP_W, GROUP_W), F32), pltpu.VMEM((SUBLANES, GROUP_W), F32),
               pltpu.VMEM((SUBLANES, LANES), F32)]
    h_fw = pl.pallas_call(
        functools.partial(_mlstm_kernel, False),
        grid=(S, nc),
        in_specs=[pl.BlockSpec((1, L, MLSTM_PAD), lambda s, c: (s, c, 0)),
                  pl.BlockSpec((1, LANES), lambda s, c: (0, 0))],
        out_specs=pl.BlockSpec((1, L, GROUP_W), lambda s, c: (s, c, 0)),
        out_shape=jax.ShapeDtypeStruct((S, T, GROUP_W), F32),
        scratch_shapes=scratch,
        compiler_params=_cparams(("parallel", "arbitrary")),
        name="mlstm_fw",
    )(pm, bias)
    return pl.pallas_call(
        functools.partial(_mlstm_kernel, True),
        grid=(S, nc),
        in_specs=[pl.BlockSpec((1, L, MLSTM_PAD), lambda s, c: (s, nc - 1 - c, 0)),
                  pl.BlockSpec((1, LANES), lambda s, c: (0, 0)),
                  pl.BlockSpec((1, L, GROUP_W), lambda s, c: (s, nc - 1 - c, 0)),
                  pl.BlockSpec((1, GROUP_W), lambda s, c: (0, 0))],
        out_specs=pl.BlockSpec((1, L, GROUP_W), lambda s, c: (s, nc - 1 - c, 0)),
        out_shape=jax.ShapeDtypeStruct((S, T, GROUP_W), BF16),
        scratch_shapes=scratch,
        compiler_params=_cparams(("parallel", "arbitrary")),
        name="mlstm_bw",
    )(pm, bias, h_fw, head_g.reshape(1, GROUP_W))


MLA_HEAD_PAD = 128
ROPE_LO = MLA_NOPE
ROPE_HALF = MLA_ROPE // 2


def _mla_proj_kernel(pa_ref, qag_ref, wuq_ref, kvag_ref, wuk_ref, wv_ref, qg_ref, kg_ref, c_ref, s_ref,
                     q_ref, k_ref, v_ref):
    pa = pa_ref[0]
    cq = pa[:, 0:256]
    ckv = pa[:, 256:384]
    kpe = pa[:, 384:512]
    cqn = (cq * lax.rsqrt(jnp.sum(cq * cq, axis=-1, keepdims=True) * (1.0 / MLA_Q_LORA) + NORM_EPS)
           * qag_ref[...]).astype(BF16)
    ckvn = (_rms(ckv, MLA_KV_LORA) * kvag_ref[...]).astype(BF16)
    q_all = _dot(cqn, wuq_ref[...])
    k_all = _dot(ckvn, wuk_ref[...])
    v_ref[0] = _dot(ckvn, wv_ref[...]).astype(v_ref.dtype)
    cos = c_ref[...]
    sin = s_ref[...]
    lane = lax.broadcasted_iota(jnp.int32, cos.shape, 1)
    first_half = lane < ROPE_LO + ROPE_HALF

    def norm_rope(x, g):
        x = x * lax.rsqrt(jnp.sum(x * x, axis=-1, keepdims=True) * (1.0 / MLA_QK) + NORM_EPS) * g
        partner = jnp.where(first_half, pltpu.roll(x, MLA_HEAD_PAD - ROPE_HALF, axis=1),
                            pltpu.roll(x, ROPE_HALF, axis=1))
        return x * cos + partner * sin

    for h in range(GROUP_HEADS):
        sl = slice(h * MLA_HEAD_PAD, (h + 1) * MLA_HEAD_PAD)
        q_ref[0, h] = (norm_rope(q_all[:, sl], qg_ref[...]) * (MLA_QK ** -0.5)).astype(q_ref.dtype)
        k_ref[0, h] = norm_rope(k_all[:, sl] + kpe, kg_ref[...]).astype(k_ref.dtype)


def _flash_kernel(q_ref, k_ref, v_ref, o_ref, m_scr, l_scr, acc_scr):
    j = pl.program_id(3)

    @pl.when(j == 0)
    def _():
        m_scr[...] = jnp.full_like(m_scr, -jnp.inf)
        l_scr[...] = jnp.zeros_like(l_scr)
        acc_scr[...] = jnp.zeros_like(acc_scr)

    v = v_ref[0]
    for hh in range(2):
        s = _dot_nt(q_ref[0, hh], k_ref[0, hh])
        m_prev = m_scr[hh]
        m_new = jnp.maximum(m_prev, jnp.max(s, axis=-1, keepdims=True))
        alpha = jnp.exp(m_prev - m_new)
        p = jnp.exp(s - m_new)
        l_scr[hh] = alpha * l_scr[hh] + jnp.sum(p, axis=-1, keepdims=True)
        acc_scr[hh] = alpha * acc_scr[hh] + _dot(p.astype(BF16), v)
        m_scr[hh] = m_new

    @pl.when(j == pl.num_programs(3) - 1)
    def _():
        o0 = acc_scr[0] / l_scr[0]
        o1 = acc_scr[1] / l_scr[1]
        lane = lax.broadcasted_iota(jnp.int32, o0.shape, 1)
        o_ref[0] = jnp.where(lane < HEAD_DIM, o0, o1).astype(o_ref.dtype)


def _rope_tables(T):
    inv = 1.0 / (ROPE_THETA ** (jnp.arange(0, MLA_ROPE, 2, dtype=F32) / MLA_ROPE))
    ang = jnp.arange(T, dtype=F32)[:, None] * inv[None, :]
    cos, sin = jnp.cos(ang), jnp.sin(ang)
    pad = jnp.zeros((T, MLA_HEAD_PAD - MLA_QK), F32)
    c = jnp.concatenate([jnp.ones((T, MLA_NOPE), F32), cos, cos, pad], axis=-1)
    s = jnp.concatenate([jnp.zeros((T, MLA_NOPE), F32), -sin, sin, pad], axis=-1)
    return c, s


def prep_mla(qa_g, w_uq, kva_g, w_ukv, q_g, k_g):
    qag = jnp.pad(qa_g, (0, 256 - MLA_Q_LORA)).reshape(1, 256)
    wq = w_uq.reshape(MLA_Q_LORA, GROUP_HEADS, MLA_QK)
    wq = jnp.pad(wq, ((0, 256 - MLA_Q_LORA), (0, 0), (0, MLA_HEAD_PAD - MLA_QK))).reshape(256, -1).astype(BF16)
    wkv = w_ukv.reshape(MLA_KV_LORA, GROUP_HEADS, MLA_NOPE + HEAD_DIM)
    wk = jnp.pad(wkv[:, :, :MLA_NOPE], ((0, 0), (0, 0), (0, MLA_HEAD_PAD - MLA_NOPE)))
    wk = wk.reshape(MLA_KV_LORA, -1).astype(BF16)
    wv = wkv[:, :, MLA_NOPE:].reshape(MLA_KV_LORA, GROUP_W).astype(BF16)
    qg = jnp.pad(q_g, (0, MLA_HEAD_PAD - MLA_QK)).reshape(1, -1)
    kg = jnp.pad(k_g, (0, MLA_HEAD_PAD - MLA_QK)).reshape(1, -1)
    return qag, wq, kva_g.reshape(1, -1), wk, wv, qg, kg


def mla_mixer(pa, qa_g, w_uq, kva_g, w_ukv, q_g, k_g, tq=512, tk=1024):
    S, T, _ = pa.shape
    tm = min(512, T)
    tq = min(tq, T)
    tk = min(tk, T)
    qag, wq, kvag, wk, wv, qg, kg = prep_mla(qa_g, w_uq, kva_g, w_ukv, q_g, k_g)
    cos, sin = _rope_tables(T)
    full = lambda a: pl.BlockSpec(a.shape, lambda s, i: (0,) * a.ndim)
    hw = GROUP_HEADS * MLA_HEAD_PAD
    q, k, v = pl.pallas_call(
        _mla_proj_kernel,
        grid=(S, T // tm),
        in_specs=[pl.BlockSpec((1, tm, MLA_PAD), lambda s, i: (s, i, 0)),
                  full(qag), full(wq), full(kvag), full(wk), full(wv), full(qg), full(kg),
                  pl.BlockSpec((tm, MLA_HEAD_PAD), lambda s, i: (i, 0)),
                  pl.BlockSpec((tm, MLA_HEAD_PAD), lambda s, i: (i, 0))],
        out_specs=[pl.BlockSpec((1, GROUP_HEADS, tm, MLA_HEAD_PAD), lambda s, i: (s, 0, i, 0)),
                   pl.BlockSpec((1, GROUP_HEADS, tm, MLA_HEAD_PAD), lambda s, i: (s, 0, i, 0)),
                   pl.BlockSpec((1, tm, GROUP_W), lambda s, i: (s, i, 0))],
        out_shape=[jax.ShapeDtypeStruct((S, GROUP_HEADS, T, MLA_HEAD_PAD), BF16),
                   jax.ShapeDtypeStruct((S, GROUP_HEADS, T, MLA_HEAD_PAD), BF16),
                   jax.ShapeDtypeStruct((S, T, GROUP_W), BF16)],
        compiler_params=_cparams(("parallel", "parallel")),
        name="mla_proj",
    )(pa, qag, wq, kvag, wk, wv, qg, kg, cos, sin)
    del hw
    return pl.pallas_call(
        _flash_kernel,
        grid=(S, 2, T // tq, T // tk),
        in_specs=[pl.BlockSpec((1, 2, tq, MLA_HEAD_PAD), lambda s, p, i, j: (s, p, i, 0)),
                  pl.BlockSpec((1, 2, tk, MLA_HEAD_PAD), lambda s, p, i, j: (s, p, j, 0)),
                  pl.BlockSpec((1, tk, 2 * HEAD_DIM), lambda s, p, i, j: (s, j, p))],
        out_specs=pl.BlockSpec((1, tq, 2 * HEAD_DIM), lambda s, p, i, j: (s, i, p)),
        out_shape=jax.ShapeDtypeStruct((S, T, GROUP_W), BF16),
        scratch_shapes=[pltpu.VMEM((2, tq, 1), F32), pltpu.VMEM((2, tq, 1), F32),
                        pltpu.VMEM((2, tq, 2 * HEAD_DIM), F32)],
        compiler_params=_cparams(("parallel", "parallel", "parallel", "arbitrary")),
        name="mla_flash",
    )(q, k, v)


HALO = SUBLANES
RWKV_SMALL_OFF = 3 * GROUP_W


def _mm(a, b):
    return _dot(a.astype(BF16), b.astype(BF16))


def _mm_nt(a, b):
    return _dot_nt(a.astype(BF16), b.astype(BF16))


def _mm_tn(a, b):
    return _dot_tn(a.astype(BF16), b.astype(BF16))


def _shifted_rows(x, prev_row, next_row):
    n = x.shape[0]
    rowi = lax.broadcasted_iota(jnp.int32, x.shape, 0)
    prev = jnp.where(rowi == 0, prev_row, pltpu.roll(x, 1, axis=0))
    nxt = jnp.where(rowi == n - 1, next_row, pltpu.roll(x, n - 1, axis=0))
    return prev, nxt


def _rwkv_kernel(rev, tb, *refs):
    if rev:
        (x_ref, hp_ref, hn_ref, mu_ref, w0_ref, ww2_ref, a0_ref, wa2_ref, wg2_ref, kk_ref, ka_ref, rk_ref,
         lng_ref, lnb_ref, yfw_ref, out_ref, z_scr, r_scr, lw_scr, k_scr, v_scr, a_scr, b_scr, y_scr) = refs
    else:
        (x_ref, hp_ref, hn_ref, mu_ref, w0_ref, ww2_ref, a0_ref, wa2_ref, kk_ref, ka_ref,
         out_ref, z_scr, r_scr, lw_scr, k_scr, v_scr, a_scr, b_scr, y_scr) = refs
    L = RWKV_CHUNK
    nch = tb // L
    nb = pl.num_programs(1)
    ib = (nb - 1 - pl.program_id(1)) if rev else pl.program_id(1)

    @pl.when(pl.program_id(1) == 0)
    def _():
        z_scr[...] = jnp.zeros_like(z_scr)

    x = x_ref[0]
    prev_row = jnp.where(ib == 0, 0.0, hp_ref[0, HALO - 1:HALO, :])
    next_row = jnp.where(ib == nb - 1, 0.0, hn_ref[0, 0:1, :])
    prev, nxt = _shifted_rows(x, prev_row, next_row)
    pf = x + mu_ref[0:1, :] * (prev - x) + mu_ref[1:2, :] * (nxt - x)
    r = pf[:, 0:GROUP_W]
    k = pf[:, GROUP_W:2 * GROUP_W]
    v = pf[:, 2 * GROUP_W:3 * GROUP_W]
    z = pf[:, RWKV_SMALL_OFF:RWKV_SMALL_OFF + LANES]
    zt = jnp.tanh(z)
    hid = _head_id((tb, GROUP_W), 1)
    kk = k * kk_ref[...]
    kk = kk * lax.rsqrt(_head_sum_bcast(kk * kk, hid) + 1e-12)

    def direction(d):
        wz = w0_ref[d:d + 1, :] + _mm(zt, ww2_ref[d])
        logw = -math.exp(-0.5) * _sigmoid(wz)
        a = _sigmoid(a0_ref[d:d + 1, :] + _mm(z, wa2_ref[d]))
        kd = k * (1.0 + (a - 1.0) * ka_ref[...])
        return logw, a, kd

    d = 1 if rev else 0
    logw, a_d, k_d = direction(d)
    r_scr[...] = r
    lw_scr[...] = logw
    k_scr[...] = k_d
    v_scr[...] = v
    a_scr[...] = -kk
    b_scr[...] = kk * a_d

    rowl = lax.broadcasted_iota(jnp.int32, (L, L), 0)
    coll = lax.broadcasted_iota(jnp.int32, (L, L), 1)
    tri = ((coll >= rowl) if rev else (coll <= rowl)).astype(F32)
    n4 = GROUP_HEADS * L
    tr = lax.broadcasted_iota(jnp.int32, (n4, n4), 0)
    tc = lax.broadcasted_iota(jnp.int32, (n4, n4), 1)
    trm = tr % L
    tcm = tc % L
    strict = (tcm > trm) if rev else (tcm < trm)
    incl = (tcm >= trm) if rev else (tcm <= trm)
    eye = (tr == tc).astype(F32)
    hid_l = _head_id((L, GROUP_W), 1)
    ones_l = jnp.ones((L, LANES), F32)

    def stack(t):
        return jnp.concatenate([jnp.where(hid_l == h, t, 0.0) for h in range(GROUP_HEADS)], axis=0)

    def chunk(c, carry):
        cc = (nch - 1 - c) if rev else c
        rows = pl.ds(pl.multiple_of(cc * L, L), L)
        lw = lw_scr[rows, :]
        cs = _dot(tri, lw, HI)
        tot = cs[0:1, :] if rev else cs[L - 1:L, :]
        ep = jnp.exp(cs)
        en = jnp.exp(-cs)
        ee = jnp.exp(tot - cs)
        kc = k_scr[rows, :]
        bc = b_scr[rows, :]
        a_s = stack(a_scr[rows, :] * jnp.exp(cs - lw))
        b_s = stack(bc * en)
        k_s = stack(kc * en)
        r_s = stack(r_scr[rows, :] * ep)
        v_s = stack(v_scr[rows, :])
        be_s = stack(bc * ee)
        ke_s = stack(kc * ee)
        aab = jnp.where(strict, _mm_nt(a_s, b_s), 0.0)
        aak = jnp.where(strict, _mm_nt(a_s, k_s), 0.0)
        arb = jnp.where(incl, _mm_nt(r_s, b_s), 0.0)
        ark = jnp.where(incl, _mm_nt(r_s, k_s), 0.0)
        inv = eye + aab
        pw = aab
        for _ in range(5):
            pw = _mm(pw, pw)
            inv = inv + _mm(inv, pw)
        z0 = z_scr[...]
        u = _mm(inv, _mm(a_s, z0) + _mm(aak, v_s))
        y_s = _mm(r_s, z0) + _mm(arb, u) + _mm(ark, v_s)
        y_scr[rows, :] = y_s[0:L] + y_s[L:2 * L] + y_s[2 * L:3 * L] + y_s[3 * L:4 * L]
        pcol = jnp.exp(_dot_tn(lw, ones_l, HI))[:, 0:1]
        z_scr[...] = pcol * z0 + _mm_tn(be_s, u) + _mm_tn(ke_s, v_s)
        return carry

    lax.fori_loop(0, nch, chunk, 0)

    if rev:
        y = yfw_ref[0] + y_scr[...]
        mean = _head_sum_bcast(y, hid) * (1.0 / HEAD_DIM)
        yc = y - mean
        var = _head_sum_bcast(yc * yc, hid) * (1.0 / HEAD_DIM)
        y = yc * lax.rsqrt(var + RWKV_GN_EPS) * lng_ref[...] + lnb_ref[...]
        _, _, k_0 = direction(0)
        bonus = _head_sum_bcast(r * (k_0 + k_d) * rk_ref[...], hid) * v
        gate = _mm(_sigmoid(z), wg2_ref[...])
        out_ref[0] = ((y + bonus) * gate).astype(out_ref.dtype)
    else:
        out_ref[0] = y_scr[...]


def rwkv_mixer(pr, mu, w0, w_w2, a0, w_a2, w_g2, k_k, k_a, r_k, ln_g, ln_b, tb=512):
    S, T, _ = pr.shape
    tb = min(tb, T)
    nb = T // tb
    hb = tb // HALO
    nh = T // HALO

    def lora_pad(w, lo):
        r = w.shape[-2]
        pad = [(0, 0)] * (w.ndim - 2) + [(lo, LANES - lo - r), (0, 0)]
        return jnp.pad(w, pad).astype(BF16)

    ww2 = lora_pad(w_w2, 0)
    wa2 = lora_pad(w_a2, 32)
    wg2 = lora_pad(w_g2, 64)
    row = lambda t: t.reshape(1, GROUP_W)
    full = lambda t: pl.BlockSpec(t.shape, lambda s, i: (0,) * t.ndim)
    scratch = [pltpu.VMEM((GROUP_W, GROUP_W), F32)] + [pltpu.VMEM((tb, GROUP_W), F32) for _ in range(7)]

    def specs(rev):
        blk = (lambda i: nb - 1 - i) if rev else (lambda i: i)
        return [pl.BlockSpec((1, tb, RWKV_IN), lambda s, i: (s, blk(i), 0)),
                pl.BlockSpec((1, HALO, RWKV_IN), lambda s, i: (s, jnp.maximum(blk(i) * hb - 1, 0), 0)),
                pl.BlockSpec((1, HALO, RWKV_IN), lambda s, i: (s, jnp.minimum((blk(i) + 1) * hb, nh - 1), 0))]

    fw_args = (pr, pr, pr, mu, w0, ww2, a0, wa2, row(k_k), row(k_a))
    y_fw = pl.pallas_call(
        functools.partial(_rwkv_kernel, False, tb),
        grid=(S, nb),
        in_specs=specs(False) + [full(t) for t in fw_args[3:]],
        out_specs=pl.BlockSpec((1, tb, GROUP_W), lambda s, i: (s, i, 0)),
        out_shape=jax.ShapeDtypeStruct((S, T, GROUP_W), F32),
        scratch_shapes=scratch,
        compiler_params=_cparams(("parallel", "arbitrary")),
        name="rwkv_fw",
    )(*fw_args)
    bw_args = (pr, pr, pr, mu, w0, ww2, a0, wa2, wg2, row(k_k), row(k_a), row(r_k), row(ln_g), row(ln_b))
    return pl.pallas_call(
        functools.partial(_rwkv_kernel, True, tb),
        grid=(S, nb),
        in_specs=specs(True) + [full(t) for t in bw_args[3:]]
        + [pl.BlockSpec((1, tb, GROUP_W), lambda s, i: (s, nb - 1 - i, 0))],
        out_specs=pl.BlockSpec((1, tb, GROUP_W), lambda s, i: (s, nb - 1 - i, 0)),
        out_shape=jax.ShapeDtypeStruct((S, T, GROUP_W), BF16),
        scratch_shapes=scratch,
        compiler_params=_cparams(("parallel", "arbitrary")),
        name="rwkv_bw",
    )(*bw_args, y_fw)


def _gelu_tanh(x):
    return 0.5 * x * (1.0 + jnp.tanh(math.sqrt(2.0 / math.pi) * (x + 0.044715 * (x * x * x))))


def _lru_kernel(rev, tb, *refs):
    if rev:
        x_ref, hp_ref, hn_ref, cw_ref, cb_ref, gw_ref, gb_ref, lam_ref, hfw_ref, out_ref, h_scr = refs
    else:
        x_ref, hp_ref, hn_ref, cw_ref, cb_ref, gw_ref, gb_ref, lam_ref, out_ref, h_scr = refs
    nb = pl.num_programs(1)
    ib = (nb - 1 - pl.program_id(1)) if rev else pl.program_id(1)

    @pl.when(pl.program_id(1) == 0)
    def _():
        h_scr[...] = jnp.zeros_like(h_scr)

    xb = x_ref[0, :, 0:GROUP_W]
    first = ib == 0
    last = ib == nb - 1
    pm2 = jnp.where(first, 0.0, hp_ref[0, HALO - 2:HALO - 1, 0:GROUP_W])
    pm1 = jnp.where(first, 0.0, hp_ref[0, HALO - 1:HALO, 0:GROUP_W])
    nx1 = jnp.where(last, 0.0, hn_ref[0, 0:1, 0:GROUP_W])
    rowi = lax.broadcasted_iota(jnp.int32, (tb, GROUP_W), 0)
    x_m1 = jnp.where(rowi == 0, pm1, pltpu.roll(xb, 1, axis=0))
    x_m2 = jnp.where(rowi == 0, pm2, jnp.where(rowi == 1, pm1, pltpu.roll(xb, 2, axis=0)))
    x_p1 = jnp.where(rowi == tb - 1, nx1, pltpu.roll(xb, tb - 1, axis=0))
    xc = (cb_ref[...] + x_m2 * cw_ref[0:1, :] + x_m1 * cw_ref[1:2, :] + xb * cw_ref[2:3, :]
          + x_p1 * cw_ref[3:4, :])
    xcb = xc.astype(BF16)
    r_gate = _sigmoid(_dot(xcb, gw_ref[0]) + gb_ref[0:1, :])
    i_gate = _sigmoid(_dot(xcb, gw_ref[1]) + gb_ref[1:2, :])
    lam = lam_ref[...]
    softplus_neg_lam = jnp.maximum(-lam, 0.0) + jnp.log(1.0 + jnp.exp(-jnp.abs(lam)))
    log_a = -LRU_C * r_gate * softplus_neg_lam
    a = jnp.exp(log_a)
    u = jnp.sqrt(1.0 - jnp.exp(2.0 * log_a)) * (i_gate * xc)

    s = 1
    while s < tb:
        shift = (tb - s) if rev else s
        valid = (rowi < tb - s) if rev else (rowi >= s)
        a_sh = pltpu.roll(a, shift, axis=0)
        u_sh = pltpu.roll(u, shift, axis=0)
        u = jnp.where(valid, a * u_sh + u, u)
        a = jnp.where(valid, a * a_sh, a)
        s *= 2
    h = a * h_scr[0:1, :] + u
    h_scr[0:1, :] = h[0:1, :] if rev else h[tb - 1:tb, :]

    if rev:
        out_ref[0] = ((hfw_ref[0] + h) * _gelu_tanh(x_ref[0, :, GROUP_W:2 * GROUP_W])).astype(out_ref.dtype)
    else:
        out_ref[0] = h


def lru_mixer(pl_in, conv_w, conv_b, gate_w, gate_b, lam, tb=512):
    S, T, _ = pl_in.shape
    tb = min(tb, T)
    nb = T // tb
    hb = tb // HALO
    nh = T // HALO
    eye = jnp.eye(GROUP_HEADS, dtype=F32)
    gw = jnp.einsum("dgnij,nm->dgnimj", gate_w, eye).reshape(2, 2, GROUP_W, GROUP_W).astype(BF16)
    full = lambda t: pl.BlockSpec(t.shape, lambda s, i: (0,) * t.ndim)

    def specs(rev):
        blk = (lambda i: nb - 1 - i) if rev else (lambda i: i)
        return [pl.BlockSpec((1, tb, LRU_IN), lambda s, i: (s, blk(i), 0)),
                pl.BlockSpec((1, HALO, LRU_IN), lambda s, i: (s, jnp.maximum(blk(i) * hb - 1, 0), 0)),
                pl.BlockSpec((1, HALO, LRU_IN), lambda s, i: (s, jnp.minimum((blk(i) + 1) * hb, nh - 1), 0))]

    def args(d):
        return (conv_w, conv_b.reshape(1, GROUP_W), gw[d], gate_b[d], lam[d].reshape(1, GROUP_W))

    scratch = [pltpu.VMEM((SUBLANES, GROUP_W), F32)]
    h_fw = pl.pallas_call(
        functools.partial(_lru_kernel, False, tb),
        grid=(S, nb),
        in_specs=specs(False) + [full(t) for t in args(0)],
        out_specs=pl.BlockSpec((1, tb, GROUP_W), lambda s, i: (s, i, 0)),
        out_shape=jax.ShapeDtypeStruct((S, T, GROUP_W), F32),
        scratch_shapes=scratch,
        compiler_params=_cparams(("parallel", "arbitrary")),
        name="lru_fw",
    )(pl_in, pl_in, pl_in, *args(0))
    return pl.pallas_call(
        functools.partial(_lru_kernel, True, tb),
        grid=(S, nb),
        in_specs=specs(True) + [full(t) for t in args(1)]
        + [pl.BlockSpec((1, tb, GROUP_W), lambda s, i: (s, nb - 1 - i, 0))],
        out_specs=pl.BlockSpec((1, tb, GROUP_W), lambda s, i: (s, nb - 1 - i, 0)),
        out_shape=jax.ShapeDtypeStruct((S, T, GROUP_W), BF16),
        scratch_shapes=scratch,
        compiler_params=_cparams(("parallel", "arbitrary")),
        name="lru_bw",
    )(pl_in, pl_in, pl_in, *args(1), h_fw)


def _out_proj_kernel(x_ref, ym_ref, ya_ref, yr_ref, yl_ref, w_ref, o_ref):
    acc = x_ref[...]
    for i, y_ref in enumerate((ym_ref, ya_ref, yr_ref, yl_ref)):
        acc = acc + _dot(y_ref[...], w_ref[i * GROUP_W:(i + 1) * GROUP_W, :])
    o_ref[...] = acc


def out_proj(x, ys, w):
    n = x.shape[0]
    tm = min(512, n)
    return pl.pallas_call(
        _out_proj_kernel,
        grid=(n // tm,),
        in_specs=[pl.BlockSpec((tm, D_MODEL), lambda i: (i, 0))]
        + [pl.BlockSpec((tm, GROUP_W), lambda i: (i, 0)) for _ in ys]
        + [pl.BlockSpec((D_MODEL, D_MODEL), lambda i: (0, 0))],
        out_specs=pl.BlockSpec((tm, D_MODEL), lambda i: (i, 0)),
        out_shape=jax.ShapeDtypeStruct((n, D_MODEL), F32),
        compiler_params=_cparams(("parallel",)),
        name="out_proj",
    )(x, *ys, w)


XA_W = XA_HEADS * XA_DIM


def _mem_kv_kernel(m_ref, g_ref, wkv_ref, kg_ref, k_ref, v_ref):
    hm = (_rms(m_ref[0], D_MODEL) * g_ref[...]).astype(BF16)
    kv = _dot(hm, wkv_ref[...])
    k = kv[:, 0:XA_W]
    hid = _head_id(k.shape, 1, XA_DIM)
    k = k * lax.rsqrt(_head_sum_bcast(k * k, hid, XA_HEADS) * (1.0 / XA_DIM) + NORM_EPS) * kg_ref[...]
    k_ref[0] = k.astype(k_ref.dtype)
    v_ref[0] = kv[:, XA_W:2 * XA_W].astype(v_ref.dtype)


def mem_kv(mem, g, wkv, k_g):
    S, M, D = mem.shape
    full = lambda t: pl.BlockSpec(t.shape, lambda s: (0,) * t.ndim)
    args = (g.reshape(1, D), wkv.astype(BF16), jnp.tile(k_g, XA_HEADS).reshape(1, XA_W))
    return pl.pallas_call(
        _mem_kv_kernel,
        grid=(S,),
        in_specs=[pl.BlockSpec((1, M, D), lambda s: (s, 0, 0))] + [full(t) for t in args],
        out_specs=[pl.BlockSpec((1, M, XA_W), lambda s: (s, 0, 0))] * 2,
        out_shape=[jax.ShapeDtypeStruct((S, M, XA_W), BF16)] * 2,
        compiler_params=_cparams(("parallel",)),
        name="mem_kv",
    )(mem, *args)


def _xattn_kernel(x_ref, g_ref, wq_ref, qg_ref, k_ref, v_ref, wo_ref, gf_ref, rt_ref, x2_ref, acc_ref, aff_ref):
    x = x_ref[0]
    h = (_rms(x, D_MODEL) * g_ref[...]).astype(BF16)
    q = _dot(h, wq_ref[...])
    hid = _head_id(q.shape, 1, XA_DIM)
    q = q * lax.rsqrt(_head_sum_bcast(q * q, hid, XA_HEADS) * (1.0 / XA_DIM) + NORM_EPS) * qg_ref[...]
    q = q * (XA_DIM ** -0.5)
    k = k_ref[0]
    v = v_ref[0]
    o = jnp.zeros(q.shape, F32)
    for hh in range(XA_HEADS):
        hm = hid == hh
        s = _dot_nt(jnp.where(hm, q, 0.0).astype(BF16), k)
        p = jnp.exp(s - jnp.max(s, axis=-1, keepdims=True))
        oh = _dot(p.astype(BF16), v) / jnp.sum(p, axis=-1, keepdims=True)
        o = jnp.where(hm, oh, o)
    x2 = x + _dot(o.astype(BF16), wo_ref[...])
    x2_ref[0] = x2
    acc_ref[0] = x2
    h3 = _rms(x2, D_MODEL) * gf_ref[...]
    logits = _dot_nt(rt_ref[...], h3, HI)
    e = jnp.exp(logits - jnp.max(logits, axis=0, keepdims=True))
    aff_ref[...] = e / jnp.sum(e, axis=0, keepdims=True)


def xattn_router(x, g_xa, wq, q_g, k, v, wo, g_ffn, router):
    S, T, D = x.shape
    tm = min(256, T)
    nb = T // tm
    args = (g_xa.reshape(1, D), wq.astype(BF16), jnp.tile(q_g, XA_HEADS).reshape(1, XA_W))
    args2 = (wo.astype(BF16), g_ffn.reshape(1, D), router.T)
    full = lambda t: pl.BlockSpec(t.shape, lambda s, i: (0,) * t.ndim)
    M = k.shape[1]
    return pl.pallas_call(
        _xattn_kernel,
        grid=(S, nb),
        in_specs=[pl.BlockSpec((1, tm, D), lambda s, i: (s, i, 0))] + [full(t) for t in args]
        + [pl.BlockSpec((1, M, XA_W), lambda s, i: (s, 0, 0))] * 2 + [full(t) for t in args2],
        out_specs=[pl.BlockSpec((1, tm, D), lambda s, i: (s, i, 0)),
                   pl.BlockSpec((1, tm, D), lambda s, i: (s, i, 0)),
                   pl.BlockSpec((N_EXPERTS, tm), lambda s, i: (0, s * nb + i))],
        out_shape=[jax.ShapeDtypeStruct((S, T, D), F32), jax.ShapeDtypeStruct((S, T, D), F32),
                   jax.ShapeDtypeStruct((N_EXPERTS, S * T), F32)],
        compiler_params=_cparams(("parallel", "parallel")),
        name="xattn_router",
    )(x, *args, k, v, *args2)


def _excl_prefix(m):
    nb = m.shape[0]
    li = lax.broadcasted_iota(jnp.int32, (LANES, LANES), 0)
    lj = lax.broadcasted_iota(jnp.int32, (LANES, LANES), 1)
    within = _dot(m.astype(BF16), (li < lj).astype(BF16))
    tot = jnp.broadcast_to(jnp.sum(m, axis=-1, keepdims=True), (nb, LANES))
    bi = lax.broadcasted_iota(jnp.int32, (nb, nb), 0)
    bj = lax.broadcasted_iota(jnp.int32, (nb, nb), 1)
    carry = _dot((bj < bi).astype(F32), tot, HI)
    carry_row = _dot_tn(tot[:, 0:SUBLANES], (bi < bj).astype(F32), HI)
    return within + carry, carry_row


def _route_select_kernel(cap, aff_ref, sel_ref, pos_ref, carry_ref):
    a = aff_ref[0]
    bits = pltpu.bitcast(a, jnp.int32)

    def bisect(i, prefix):
        cand = prefix | jnp.left_shift(jnp.int32(1), 30 - i)
        ge = (bits >= cand).astype(F32)
        cnt = jnp.sum(jnp.sum(ge, axis=-1, keepdims=True), axis=0, keepdims=True)
        return jnp.where(cnt >= cap, cand, prefix)

    thr = lax.fori_loop(0, 31, bisect, jnp.zeros((1, 1), jnp.int32))
    gt = bits > thr
    eq = (bits == thr).astype(F32)
    n_gt = jnp.sum(jnp.sum(gt.astype(F32), axis=-1, keepdims=True), axis=0, keepdims=True)
    eq_rank, _ = _excl_prefix(eq)
    sel = jnp.where(gt, 1.0, jnp.where(eq_rank < cap - n_gt, eq, 0.0))
    pos, carry_row = _excl_prefix(sel)
    sel_ref[0] = sel
    pos_ref[0] = pos
    carry_ref[0] = carry_row.astype(jnp.int32)


def _route_compact_kernel(base, carry_ref, aff_ref, sel_ref, pos_ref, idx_ref, gate_ref):
    e = pl.program_id(0)
    nb = aff_ref.shape[1]
    nj = idx_ref.shape[1]
    idx_ref[...] = jnp.zeros_like(idx_ref)
    gate_ref[...] = jnp.zeros_like(gate_ref)
    sub = lax.broadcasted_iota(jnp.int32, (LANES, LANES), 0).astype(F32)
    s8 = lax.broadcasted_iota(jnp.int32, (SUBLANES, LANES), 0)
    lane_f = lax.broadcasted_iota(jnp.int32, (SUBLANES, LANES), 1).astype(F32)

    def row(b, carry):
        rows = pl.ds(b, 1)
        sel = sel_ref[0, rows, :]
        pos = pos_ref[0, rows, :]
        lhs = jnp.where(s8 == 0, lane_f, jnp.where(s8 == 1, 1.0, jnp.where(s8 == 2, aff_ref[0, rows, :], 0.0)))
        j0 = carry_ref[e, b] // LANES
        for dj in range(2):
            j = jnp.minimum(j0 + dj, nj - 1)
            onehot = jnp.where(pos - jnp.asarray((j0 + dj) * LANES, F32) == sub, sel, 0.0)
            res = _dot_nt(lhs, onehot, HI)
            tok = res[0:1, :] + jnp.asarray(base + b * LANES, F32) * res[1:2, :]
            idx_ref[0, pl.ds(j, 1), :] += tok.astype(jnp.int32)
            gate_ref[0, pl.ds(j, 1), :] += res[2:3, :]
        return carry

    lax.fori_loop(0, nb, row, 0)


def route(aff, cap, base):
    E, ng = aff.shape
    nb = ng // LANES
    nj = cap // LANES
    aff3 = aff.reshape(E, nb, LANES)
    blk = pl.BlockSpec((1, nb, LANES), lambda e: (e, 0, 0))
    sel, pos, carry = pl.pallas_call(
        functools.partial(_route_select_kernel, cap),
        grid=(E,),
        in_specs=[blk],
        out_specs=[blk, blk, pl.BlockSpec((1, SUBLANES, nb), lambda e: (e, 0, 0))],
        out_shape=[jax.ShapeDtypeStruct((E, nb, LANES), F32), jax.ShapeDtypeStruct((E, nb, LANES), F32),
                   jax.ShapeDtypeStruct((E, SUBLANES, nb), jnp.int32)],
        compiler_params=_cparams(("parallel",)),
        name="route_select",
    )(aff3)
    oblk = pl.BlockSpec((1, nj, LANES), lambda e, c: (e, 0, 0))
    blk2 = pl.BlockSpec((1, nb, LANES), lambda e, c: (e, 0, 0))
    idx, gate = pl.pallas_call(
        functools.partial(_route_compact_kernel, base),
        grid_spec=pltpu.PrefetchScalarGridSpec(
            num_scalar_prefetch=1, grid=(E,), in_specs=[blk2, blk2, blk2], out_specs=[oblk, oblk]),
        out_shape=[jax.ShapeDtypeStruct((E, nj, LANES), jnp.int32), jax.ShapeDtypeStruct((E, nj, LANES), F32)],
        compiler_params=_cparams(("parallel",)),
        name="route_compact",
    )(carry[:, 0, :], aff3, sel, pos)
    return idx.reshape(E, cap), gate.reshape(E, cap)


MOE_ROWS = 256


def _moe_kernel(idx_ref, gate_ref, g_ref, wg_ref, wu_ref, wd_ref, x_hbm, acc_in_hbm, out_hbm,
                xbuf, abuf, sem_x, sem_a, sem_o):
    del acc_in_hbm
    nrow = xbuf.shape[0]

    def x_copy(i):
        return pltpu.make_async_copy(x_hbm.at[pl.ds(idx_ref[0, 0, i], 1), :], xbuf.at[pl.ds(i, 1), :], sem_x)

    def a_copy(i):
        return pltpu.make_async_copy(out_hbm.at[pl.ds(idx_ref[0, 0, i], 1), :], abuf.at[pl.ds(i, 1), :], sem_a)

    def o_copy(i):
        return pltpu.make_async_copy(abuf.at[pl.ds(i, 1), :], out_hbm.at[pl.ds(idx_ref[0, 0, i], 1), :], sem_o)

    def each(fn):
        def body(i, c):
            fn(i)
            return c
        lax.fori_loop(0, nrow, body, 0)

    each(lambda i: (x_copy(i).start(), a_copy(i).start()))
    each(lambda i: x_copy(i).wait())
    h = (_rms(xbuf[...], D_MODEL) * g_ref[...]).astype(BF16)
    gate_proj = _dot(h, wg_ref[0])
    hid = gate_proj * _sigmoid(gate_proj) * _dot(h, wu_ref[0])
    y = _dot(hid.astype(BF16), wd_ref[0]) * gate_ref[...]
    each(lambda i: a_copy(i).wait())
    abuf[...] = abuf[...] + y
    each(lambda i: o_copy(i).start())
    each(lambda i: o_copy(i).wait())


def moe_ffn(x2, acc, idx, gate, g_ffn, w_gate, w_up, w_down):
    n, D = x2.shape
    E, C = idx.shape
    rows = math.gcd(C, MOE_ROWS)
    nc = C // rows
    ff = w_gate.shape[-1]
    idx3 = idx.reshape(E * nc, 1, rows)
    gate2 = gate.reshape(E * C, 1)
    any_spec = pl.BlockSpec(memory_space=pl.ANY)
    return pl.pallas_call(
        _moe_kernel,
        grid=(E, nc),
        in_specs=[pl.BlockSpec((1, 1, rows), lambda e, c: (e * nc + c, 0, 0), memory_space=pltpu.SMEM),
                  pl.BlockSpec((rows, 1), lambda e, c: (e * nc + c, 0)),
                  pl.BlockSpec((1, D), lambda e, c: (0, 0)),
                  pl.BlockSpec((1, D, ff), lambda e, c: (e, 0, 0)),
                  pl.BlockSpec((1, D, ff), lambda e, c: (e, 0, 0)),
                  pl.BlockSpec((1, ff, D), lambda e, c: (e, 0, 0)),
                  any_spec, any_spec],
        out_specs=any_spec,
        out_shape=jax.ShapeDtypeStruct((n, D), F32),
        scratch_shapes=[pltpu.VMEM((rows, D), F32), pltpu.VMEM((rows, D), F32),
                        pltpu.SemaphoreType.DMA, pltpu.SemaphoreType.DMA, pltpu.SemaphoreType.DMA],
        input_output_aliases={7: 0},
        compiler_params=_cparams(("arbitrary", "arbitrary")),
        name="moe_ffn",
    )(idx3, gate2, g_ffn.reshape(1, D), w_gate, w_up, w_down, x2, acc)


def kernel(x_prompt, x_sample, mem_prompt, mem_sample, norm_mix_g, w_in, mlstm_gate_b, mlstm_head_g, mla_qa_g, mla_w_uq, mla_kva_g, mla_w_ukv, mla_q_g, mla_k_g, rwkv_mu, rwkv_w0, rwkv_w_w2, rwkv_a0, rwkv_w_a2, rwkv_w_g2, rwkv_k_k, rwkv_k_a, rwkv_r_k, rwkv_ln_g, rwkv_ln_b, lru_conv_w, lru_conv_b, lru_gate_w, lru_gate_b, lru_lambda, w_out, norm_xa_g, norm_mem_g, xa_wq, xa_wkv, xa_q_g, xa_k_g, xa_wo, norm_ffn_g, moe_router, moe_w_gate, moe_w_up, moe_w_down):
    assert x_prompt.shape[1] == x_sample.shape[1]
    n_prompt = x_prompt.shape[0]
    x = jnp.concatenate([x_prompt, x_sample], axis=0)
    mem = jnp.concatenate([mem_prompt, mem_sample], axis=0)
    S, T, D = x.shape
    N = S * T
    tok_prompt = n_prompt * T
    tok_sample = N - tok_prompt
    cap_prompt = max(1, (CAPACITY_FACTOR * tok_prompt) // N_EXPERTS)
    cap_sample = max(1, (CAPACITY_FACTOR * tok_sample) // N_EXPERTS)
    w_in_p = prep_w_in(w_in)
    w_out_b = w_out.astype(BF16)
    w_gate_b = moe_w_gate.astype(BF16)
    w_up_b = moe_w_up.astype(BF16)
    w_down_b = moe_w_down.astype(BF16)
    x = x.reshape(N, D)
    for l in range(w_in.shape[0]):
        pm, pa, pr, pl_ = in_proj(x, norm_mix_g[l].reshape(1, D), w_in_p[l])
        y_m = mlstm_mixer(pm.reshape(S, T, -1), mlstm_gate_b[l], mlstm_head_g[l])
        y_a = mla_mixer(pa.reshape(S, T, -1), mla_qa_g[l], mla_w_uq[l], mla_kva_g[l], mla_w_ukv[l], mla_q_g[l],
                        mla_k_g[l])
        y_r = rwkv_mixer(pr.reshape(S, T, -1), rwkv_mu[l], rwkv_w0[l], rwkv_w_w2[l], rwkv_a0[l], rwkv_w_a2[l],
                         rwkv_w_g2[l], rwkv_k_k[l], rwkv_k_a[l], rwkv_r_k[l], rwkv_ln_g[l], rwkv_ln_b[l])
        y_l = lru_mixer(pl_.reshape(S, T, -1), lru_conv_w[l], lru_conv_b[l], lru_gate_w[l], lru_gate_b[l],
                        lru_lambda[l])
        x1 = out_proj(x, [y.reshape(N, GROUP_W) for y in (y_m, y_a, y_r, y_l)], w_out_b[l])
        k_mem, v_mem = mem_kv(mem, norm_mem_g[l], xa_wkv[l], xa_k_g[l])
        x2, acc, aff = xattn_router(x1.reshape(S, T, D), norm_xa_g[l], xa_wq[l], xa_q_g[l], k_mem, v_mem,
                                    xa_wo[l], norm_ffn_g[l], moe_router[l])
        idx_p, gate_p = route(aff[:, :tok_prompt], cap_prompt, 0)
        idx_s, gate_s = route(aff[:, tok_prompt:], cap_sample, tok_prompt)
        idx = jnp.concatenate([idx_p, idx_s], axis=1)
        gate = jnp.concatenate([gate_p, gate_s], axis=1)
        x = moe_ffn(x2.reshape(N, D), acc.reshape(N, D), idx, gate, norm_ffn_g[l], w_gate_b[l], w_up_b[l],
                    w_down_b[l])
    x = x.reshape(S, T, D)
    return x[:n_prompt], x[n_prompt:]
```

```python
import functools
import math

import jax
import jax.numpy as jnp
from jax import lax
from jax.experimental import pallas as pl
from jax.experimental.pallas import tpu as pltpu

F32 = jnp.float32
BF16 = jnp.bfloat16
HI = lax.Precision.HIGHEST

D_MODEL = 1024
NORM_EPS = 1e-6
GROUP_W = 256
HEAD_DIM = 64
GROUP_HEADS = 4

MLSTM_CHUNK = 128
MLSTM_IN = 4 * GROUP_W + 4 * GROUP_HEADS
MLSTM_PAD = 1152

MLA_Q_LORA = 192
MLA_KV_LORA = 128
MLA_NOPE = 64
MLA_ROPE = 32
MLA_QK = MLA_NOPE + MLA_ROPE
MLA_IN = MLA_Q_LORA + MLA_KV_LORA + MLA_ROPE
MLA_PAD = 512
ROPE_THETA = 10000.0
LOG2E = math.log2(math.e)

RWKV_IN = 3 * GROUP_W + 32 + 32 + 64
RWKV_GN_EPS = 64e-5
RWKV_CHUNK = 64

LRU_CONV = 4
LRU_C = 8.0
LRU_IN = 2 * GROUP_W

N_EXPERTS = 16
CAPACITY_FACTOR = 2
XA_HEADS = 4
XA_DIM = 64

VMEM_LIMIT_BYTES = 56 * 1024 * 1024
SUBLANES = 8
LANES = 128


def _cparams(sem):
    return pltpu.CompilerParams(dimension_semantics=sem, vmem_limit_bytes=VMEM_LIMIT_BYTES)


def _dot(a, b, prec=None):
    return jnp.dot(a, b, preferred_element_type=F32, precision=prec)


def _dot_nt(a, b, prec=None):
    return lax.dot_general(a, b, (((1,), (1,)), ((), ())), preferred_element_type=F32, precision=prec)


def _dot_tn(a, b, prec=None):
    return lax.dot_general(a, b, (((0,), (0,)), ((), ())), preferred_element_type=F32, precision=prec)


def _sigmoid(x):
    return 1.0 / (1.0 + jnp.exp(-x))


def _log_sigmoid(x):
    return jnp.minimum(x, 0.0) - jnp.log(1.0 + jnp.exp(-jnp.abs(x)))


def _rms(x, n):
    return x * lax.rsqrt(jnp.sum(x * x, axis=-1, keepdims=True) * (1.0 / n) + NORM_EPS)


def _head_id(shape, axis, hd=HEAD_DIM):
    return lax.broadcasted_iota(jnp.int32, shape, axis) // hd


def _head_sum_bcast(x, hid, nheads=GROUP_HEADS):
    out = jnp.zeros_like(x)
    for h in range(nheads):
        hm = hid == h
        s = jnp.sum(jnp.where(hm, x, 0.0), axis=-1, keepdims=True)
        out = jnp.where(hm, s, out)
    return out


IN_SPLITS = (MLSTM_PAD, MLA_PAD, RWKV_IN, LRU_IN)
IN_TOTAL = sum(IN_SPLITS)


def _in_proj_kernel(x_ref, g_ref, w_ref, pm_ref, pa_ref, pr_ref, pl_ref):
    x = x_ref[...]
    h = (_rms(x, D_MODEL) * g_ref[...]).astype(BF16)
    off = 0
    for ref, width in zip((pm_ref, pa_ref, pr_ref, pl_ref), IN_SPLITS):
        ref[...] = _dot(h, w_ref[:, off:off + width])
        off += width


def in_proj(x, g, w):
    n = x.shape[0]
    tm = min(256, n)
    return pl.pallas_call(
        _in_proj_kernel,
        grid=(n // tm,),
        in_specs=[
            pl.BlockSpec((tm, D_MODEL), lambda i: (i, 0)),
            pl.BlockSpec((1, D_MODEL), lambda i: (0, 0)),
            pl.BlockSpec((D_MODEL, IN_TOTAL), lambda i: (0, 0)),
        ],
        out_specs=[pl.BlockSpec((tm, wd), lambda i: (i, 0)) for wd in IN_SPLITS],
        out_shape=[jax.ShapeDtypeStruct((n, wd), F32) for wd in IN_SPLITS],
        compiler_params=_cparams(("parallel",)),
        name="in_proj",
    )(x, g, w)


def prep_w_in(w_in):
    s1 = MLSTM_IN
    s2 = s1 + MLA_IN
    s3 = s2 + RWKV_IN
    nl = w_in.shape[0]

    def z(width):
        return jnp.zeros((nl, D_MODEL, width), F32)

    w_m = jnp.concatenate([w_in[:, :, :s1], z(MLSTM_PAD - MLSTM_IN)], axis=-1)
    a = w_in[:, :, s1:s2]
    w_a = jnp.concatenate([
        a[:, :, :MLA_Q_LORA], z(256 - MLA_Q_LORA),
        a[:, :, MLA_Q_LORA:MLA_Q_LORA + MLA_KV_LORA],
        z(MLA_NOPE), a[:, :, MLA_Q_LORA + MLA_KV_LORA:], z(128 - MLA_NOPE - MLA_ROPE),
    ], axis=-1)
    return jnp.concatenate([w_m, w_a, w_in[:, :, s2:s3], w_in[:, :, s3:]], axis=-1).astype(BF16)


def _mlstm_kernel(rev, *refs):
    if rev:
        pm_ref, bias_ref, hfw_ref, hg_ref, out_ref, ct_scr, n_scr, m_scr = refs
    else:
        pm_ref, bias_ref, out_ref, ct_scr, n_scr, m_scr = refs
    L = MLSTM_CHUNK

    @pl.when(pl.program_id(1) == 0)
    def _():
        ct_scr[...] = jnp.zeros_like(ct_scr)
        n_scr[...] = jnp.zeros_like(n_scr)
        m_scr[...] = jnp.zeros_like(m_scr)

    p = pm_ref[0]
    q = p[:, 0:GROUP_W]
    k = p[:, GROUP_W:2 * GROUP_W] * (HEAD_DIM ** -0.5)
    v = p[:, 2 * GROUP_W:3 * GROUP_W]
    gt = p[:, 4 * GROUP_W:4 * GROUP_W + LANES] + bias_ref[...]
    ioff, foff = (2 * GROUP_HEADS, 3 * GROUP_HEADS) if rev else (0, GROUP_HEADS)
    logf = _log_sigmoid(gt)
    row = lax.broadcasted_iota(jnp.int32, (L, L), 0)
    col = lax.broadcasted_iota(jnp.int32, (L, L), 1)
    tri = (col >= row) if rev else (col <= row)
    bcum = _dot(tri.astype(F32), logf, HI)
    bcum_t = bcum.T
    gt_t = gt.T
    hid = _head_id((L, GROUP_W), 1)
    hid_row = hid[0:1, :]
    rid = _head_id((GROUP_W, GROUP_W), 0)
    cid = _head_id((GROUP_W, GROUP_W), 1)

    qb = q.astype(BF16)
    kb = k.astype(BF16)
    vb = v.astype(BF16)
    n_row = n_scr[0:1, :]
    qc = _dot(qb, ct_scr[...].astype(BF16))
    qn = _head_sum_bcast(q * n_row, hid)

    acc = jnp.zeros((L, GROUP_W), F32)
    wg_full = jnp.zeros((L, GROUP_W), F32)
    cs_row = jnp.zeros((1, GROUP_W), F32)
    cs_full = jnp.zeros((GROUP_W, GROUP_W), F32)
    for h in range(GROUP_HEADS):
        hm = hid == h
        i_row = gt_t[ioff + h:ioff + h + 1, :]
        i_col = gt[:, ioff + h:ioff + h + 1]
        b_row = bcum_t[foff + h:foff + h + 1, :]
        b_col = bcum[:, foff + h:foff + h + 1]
        m_st = m_scr[h:h + 1, 0:1]
        dmat = jnp.where(tri, b_col - b_row + i_row, -jnp.inf)
        inter = b_col + m_st
        m_row = jnp.maximum(inter, jnp.max(dmat, axis=-1, keepdims=True))
        s = _dot_nt(jnp.where(hm, q, 0.0).astype(BF16), kb) * jnp.exp(dmat - m_row)
        w_inter = jnp.exp(inter - m_row)
        num = _dot(s.astype(BF16), vb) + w_inter * qc
        den = jnp.sum(s, axis=-1, keepdims=True) + w_inter * qn
        hh = num / jnp.maximum(jnp.abs(den), jnp.exp(-m_row))
        acc = jnp.where(hm, hh, acc)

        b_end = b_col[0:1, :] if rev else b_col[L - 1:L, :]
        g = b_end - b_col + i_col
        m_new = jnp.maximum(b_end + m_st, jnp.max(g, axis=0, keepdims=True))
        wg_full = jnp.where(hm, jnp.exp(g - m_new), wg_full)
        cs = jnp.exp(b_end + m_st - m_new)
        cs_row = jnp.where(hid_row == h, cs, cs_row)
        cs_full = jnp.where(rid == h, cs, cs_full)
        m_scr[h:h + 1, :] = jnp.broadcast_to(m_new, (1, LANES))

    kw = k * wg_full
    ct_new = cs_full * ct_scr[...] + _dot_tn(kw.astype(BF16), vb)
    ct_scr[...] = jnp.where(rid == cid, ct_new, 0.0)
    n_scr[0:1, :] = cs_row * n_row + jnp.sum(kw, axis=0, keepdims=True)

    if rev:
        hs = hfw_ref[0] + acc
        ms = _head_sum_bcast(hs * hs, hid) * (1.0 / HEAD_DIM)
        y = hs * lax.rsqrt(ms + NORM_EPS) * hg_ref[...]
        o = p[:, 3 * GROUP_W:4 * GROUP_W]
        out_ref[0] = (y * _sigmoid(o)).astype(out_ref.dtype)
    else:
        out_ref[0] = acc


def mlstm_mixer(pm, gate_b, head_g):
    S, T, _ = pm.shape
    L = MLSTM_CHUNK
    nc = T // L
    bias = jnp.zeros((1, LANES), F32).at[0, :4 * GROUP_HEADS].set(gate_b)
    scratch = [pltpu.VMEM((GROUP_W, GROUP_W), F32), pltpu.VMEM((SUBLANES, GROUP_W), F32),
               pltpu.VMEM((SUBLANES, LANES), F32)]
    h_fw = pl.pallas_call(
        functools.partial(_mlstm_kernel, False),
        grid=(S, nc),
        in_specs=[pl.BlockSpec((1, L, MLSTM_PAD), lambda s, c: (s, c, 0)),
                  pl.BlockSpec((1, LANES), lambda s, c: (0, 0))],
        out_specs=pl.BlockSpec((1, L, GROUP_W), lambda s, c: (s, c, 0)),
        out_shape=jax.ShapeDtypeStruct((S, T, GROUP_W), F32),
        scratch_shapes=scratch,
        compiler_params=_cparams(("parallel", "arbitrary")),
        name="mlstm_fw",
    )(pm, bias)
    return pl.pallas_call(
        functools.partial(_mlstm_kernel, True),
        grid=(S, nc),
        in_specs=[pl.BlockSpec((1, L, MLSTM_PAD), lambda s, c: (s, nc - 1 - c, 0)),
                  pl.BlockSpec((1, LANES), lambda s, c: (0, 0)),
                  pl.BlockSpec((1, L, GROUP_W), lambda s, c: (s, nc - 1 - c, 0)),
                  pl.BlockSpec((1, GROUP_W), lambda s, c: (0, 0))],
        out_specs=pl.BlockSpec((1, L, GROUP_W), lambda s, c: (s, nc - 1 - c, 0)),
        out_shape=jax.ShapeDtypeStruct((S, T, GROUP_W), BF16),
        scratch_shapes=scratch,
        compiler_params=_cparams(("parallel", "arbitrary")),
        name="mlstm_bw",
    )(pm, bias, h_fw, head_g.reshape(1, GROUP_W))


MLA_HEAD_PAD = 128
ROPE_LO = MLA_NOPE
ROPE_HALF = MLA_ROPE // 2
K_NORM_LANE = MLA_QK
Q_NORM_LANE = MLA_QK + 1
NORM_INFLATE = 1.0 + 2.0 ** -6
MAX_LAG_LOG2 = 100.0


def _mla_proj_kernel(pa_ref, qag_ref, wuq_ref, kvag_ref, wuk_ref, wv_ref, qg_ref, kg_ref, c_ref, s_ref,
                     q_ref, k_ref, v_ref):
    pa = pa_ref[0]
    cq = pa[:, 0:256]
    ckv = pa[:, 256:384]
    kpe = pa[:, 384:512]
    cqn = (cq * lax.rsqrt(jnp.sum(cq * cq, axis=-1, keepdims=True) * (1.0 / MLA_Q_LORA) + NORM_EPS)
           * qag_ref[...]).astype(BF16)
    ckvn = (_rms(ckv, MLA_KV_LORA) * kvag_ref[...]).astype(BF16)
    q_all = _dot(cqn, wuq_ref[...])
    k_all = _dot(ckvn, wuk_ref[...])
    v_all = _dot(ckvn, wv_ref[...])
    cos = c_ref[...]
    sin = s_ref[...]
    lane = lax.broadcasted_iota(jnp.int32, cos.shape, 1)
    first_half = lane < ROPE_LO + ROPE_HALF

    def norm_rope(x, g):
        x = x * lax.rsqrt(jnp.sum(x * x, axis=-1, keepdims=True) * (1.0 / MLA_QK) + NORM_EPS) * g
        partner = jnp.where(first_half, pltpu.roll(x, MLA_HEAD_PAD - ROPE_HALF, axis=1),
                            pltpu.roll(x, ROPE_HALF, axis=1))
        return x * cos + partner * sin

    def with_norm(x, norm_lane):
        xb = x.astype(BF16)
        xf = xb.astype(F32)
        norm = jnp.sqrt(jnp.sum(xf * xf, axis=-1, keepdims=True)) * NORM_INFLATE
        return jnp.where(lane == norm_lane, norm.astype(BF16), xb)

    for h in range(GROUP_HEADS):
        sl = slice(h * MLA_HEAD_PAD, (h + 1) * MLA_HEAD_PAD)
        q_ref[0, h] = with_norm(norm_rope(q_all[:, sl], qg_ref[...]) * (MLA_QK ** -0.5 * LOG2E), Q_NORM_LANE)
        k_ref[0, h] = with_norm(norm_rope(k_all[:, sl] + kpe, kg_ref[...]), K_NORM_LANE)
        v_ref[0, h] = jnp.where(lane < HEAD_DIM, v_all[:, sl], 1.0).astype(v_ref.dtype)


def _flash_kernel(q_ref, k_ref, v_ref, o_ref, m_scr, acc_scr):
    j = pl.program_id(3)

    for hh in range(2):
        k_norm_max = jnp.max(k_ref[0, hh], axis=0, keepdims=True)[:, K_NORM_LANE:K_NORM_LANE + 1].astype(F32)
        q_norm = q_ref[0, hh, :, Q_NORM_LANE:Q_NORM_LANE + 1].astype(F32)

        @pl.when(j == 0)
        def _():
            m_scr[hh] = -(q_norm * k_norm_max)
            acc_scr[hh] = jnp.zeros(acc_scr.shape[1:], acc_scr.dtype)

        lagged_ok = jnp.max(q_norm * k_norm_max - m_scr[hh]) <= MAX_LAG_LOG2

        @pl.when(lagged_ok)
        def _():
            m_prev = m_scr[hh]
            s = _dot_nt(q_ref[0, hh], k_ref[0, hh])
            p = jnp.exp2(s - m_prev).astype(BF16)
            m_new = jnp.maximum(m_prev, jnp.max(s, axis=-1, keepdims=True))
            acc_scr[hh] = (acc_scr[hh] + _dot(p, v_ref[0, hh])) * jnp.exp2(m_prev - m_new)
            m_scr[hh] = m_new

        @pl.when(jnp.logical_not(lagged_ok))
        def _():
            m_prev = m_scr[hh]
            s = _dot_nt(q_ref[0, hh], k_ref[0, hh])
            m_new = jnp.maximum(m_prev, jnp.max(s, axis=-1, keepdims=True))
            p = jnp.exp2(s - m_new).astype(BF16)
            acc_scr[hh] = jnp.exp2(m_prev - m_new) * acc_scr[hh] + _dot(p, v_ref[0, hh])
            m_scr[hh] = m_new

    @pl.when(j == pl.num_programs(3) - 1)
    def _():
        a0 = acc_scr[0]
        a1 = acc_scr[1]
        o0 = a0 / a0[:, HEAD_DIM:HEAD_DIM + 1]
        o1 = a1 / a1[:, HEAD_DIM:HEAD_DIM + 1]
        lane = lax.broadcasted_iota(jnp.int32, o0.shape, 1)
        o_ref[0] = jnp.where(lane < HEAD_DIM, o0, pltpu.roll(o1, HEAD_DIM, axis=1)).astype(o_ref.dtype)


def _rope_tables(T):
    inv = 1.0 / (ROPE_THETA ** (jnp.arange(0, MLA_ROPE, 2, dtype=F32) / MLA_ROPE))
    ang = jnp.arange(T, dtype=F32)[:, None] * inv[None, :]
    cos, sin = jnp.cos(ang), jnp.sin(ang)
    pad = jnp.zeros((T, MLA_HEAD_PAD - MLA_QK), F32)
    c = jnp.concatenate([jnp.ones((T, MLA_NOPE), F32), cos, cos, pad], axis=-1)
    s = jnp.concatenate([jnp.zeros((T, MLA_NOPE), F32), -sin, sin, pad], axis=-1)
    return c, s


def prep_mla(qa_g, w_uq, kva_g, w_ukv, q_g, k_g):
    qag = jnp.pad(qa_g, (0, 256 - MLA_Q_LORA)).reshape(1, 256)
    wq = w_uq.reshape(MLA_Q_LORA, GROUP_HEADS, MLA_QK)
    wq = jnp.pad(wq, ((0, 256 - MLA_Q_LORA), (0, 0), (0, MLA_HEAD_PAD - MLA_QK))).reshape(256, -1).astype(BF16)
    wkv = w_ukv.reshape(MLA_KV_LORA, GROUP_HEADS, MLA_NOPE + HEAD_DIM)
    wk = jnp.pad(wkv[:, :, :MLA_NOPE], ((0, 0), (0, 0), (0, MLA_HEAD_PAD - MLA_NOPE)))
    wk = wk.reshape(MLA_KV_LORA, -1).astype(BF16)
    wv = jnp.pad(wkv[:, :, MLA_NOPE:], ((0, 0), (0, 0), (0, MLA_HEAD_PAD - HEAD_DIM)))
    wv = wv.reshape(MLA_KV_LORA, -1).astype(BF16)
    qg = jnp.pad(q_g, (0, MLA_HEAD_PAD - MLA_QK)).reshape(1, -1)
    kg = jnp.pad(k_g, (0, MLA_HEAD_PAD - MLA_QK)).reshape(1, -1)
    return qag, wq, kva_g.reshape(1, -1), wk, wv, qg, kg


def mla_mixer(pa, qa_g, w_uq, kva_g, w_ukv, q_g, k_g, tq=1024, tk=2048):
    S, T, _ = pa.shape
    tm = min(512, T)
    tq = min(tq, T)
    tk = min(tk, T)
    qag, wq, kvag, wk, wv, qg, kg = prep_mla(qa_g, w_uq, kva_g, w_ukv, q_g, k_g)
    cos, sin = _rope_tables(T)
    full = lambda a: pl.BlockSpec(a.shape, lambda s, i: (0,) * a.ndim)
    head_blk = pl.BlockSpec((1, GROUP_HEADS, tm, MLA_HEAD_PAD), lambda s, i: (s, 0, i, 0))
    head_shape = jax.ShapeDtypeStruct((S, GROUP_HEADS, T, MLA_HEAD_PAD), BF16)
    q, k, v = pl.pallas_call(
        _mla_proj_kernel,
        grid=(S, T // tm),
        in_specs=[pl.BlockSpec((1, tm, MLA_PAD), lambda s, i: (s, i, 0)),
                  full(qag), full(wq), full(kvag), full(wk), full(wv), full(qg), full(kg),
                  pl.BlockSpec((tm, MLA_HEAD_PAD), lambda s, i: (i, 0)),
                  pl.BlockSpec((tm, MLA_HEAD_PAD), lambda s, i: (i, 0))],
        out_specs=[head_blk, head_blk, head_blk],
        out_shape=[head_shape, head_shape, head_shape],
        compiler_params=_cparams(("parallel", "parallel")),
        name="mla_proj",
    )(pa, qag, wq, kvag, wk, wv, qg, kg, cos, sin)
    kv_blk = pl.BlockSpec((1, 2, tk, MLA_HEAD_PAD), lambda s, p, i, j: (s, p, j, 0))
    return pl.pallas_call(
        _flash_kernel,
        grid=(S, 2, T // tq, T // tk),
        in_specs=[pl.BlockSpec((1, 2, tq, MLA_HEAD_PAD), lambda s, p, i, j: (s, p, i, 0)), kv_blk, kv_blk],
        out_specs=pl.BlockSpec((1, tq, 2 * HEAD_DIM), lambda s, p, i, j: (s, i, p)),
        out_shape=jax.ShapeDtypeStruct((S, T, GROUP_W), BF16),
        scratch_shapes=[pltpu.VMEM((2, tq, 1), F32), pltpu.VMEM((2, tq, MLA_HEAD_PAD), F32)],
        compiler_params=_cparams(("parallel", "parallel", "parallel", "arbitrary")),
        name="mla_flash",
    )(q, k, v)


HALO = SUBLANES
RWKV_SMALL_OFF = 3 * GROUP_W


def _mm(a, b):
    return _dot(a.astype(BF16), b.astype(BF16))


def _mm_nt(a, b):
    return _dot_nt(a.astype(BF16), b.astype(BF16))


def _mm_tn(a, b):
    return _dot_tn(a.astype(BF16), b.astype(BF16))


def _shifted_rows(x, prev_row, next_row):
    n = x.shape[0]
    rowi = lax.broadcasted_iota(jnp.int32, x.shape, 0)
    prev = jnp.where(rowi == 0, prev_row, pltpu.roll(x, 1, axis=0))
    nxt = jnp.where(rowi == n - 1, next_row, pltpu.roll(x, n - 1, axis=0))
    return prev, nxt


def _rwkv_prep(x, prev_row, next_row, mu_ref, kk_ref):
    prev, nxt = _shifted_rows(x, prev_row, next_row)
    pf = x + mu_ref[0:1, :] * (prev - x) + mu_ref[1:2, :] * (nxt - x)
    r = pf[:, 0:GROUP_W]
    k = pf[:, GROUP_W:2 * GROUP_W]
    v = pf[:, 2 * GROUP_W:3 * GROUP_W]
    z = pf[:, RWKV_SMALL_OFF:RWKV_SMALL_OFF + LANES]
    kk = k * kk_ref[...]
    kk = kk * lax.rsqrt(_head_sum_bcast(kk * kk, _head_id(kk.shape, 1)) + 1e-12)
    return r, k, v, z, kk


def _rwkv_direction(d, z, zt, k, w0_ref, ww2_ref, a0_ref, wa2_ref, ka_ref):
    wz = w0_ref[d:d + 1, :] + _mm(zt, ww2_ref[d])
    logw = -math.exp(-0.5) * _sigmoid(wz)
    a = _sigmoid(a0_ref[d:d + 1, :] + _mm(z, wa2_ref[d]))
    kd = k * (1.0 + (a - 1.0) * ka_ref[...])
    return logw, a, kd


def _rwkv_scan_kernel(tb, xf_ref, hpf_ref, hnf_ref, xb_ref, hpb_ref, hnb_ref, mu_ref, w0_ref, ww2_ref, a0_ref,
                      wa2_ref, kk_ref, ka_ref, yf_ref, yb_ref, z_scr, r_scr, lw_scr, k_scr, v_scr, a_scr, b_scr):
    L = RWKV_CHUNK
    nch = tb // L
    nb = pl.num_programs(1)
    i = pl.program_id(1)

    @pl.when(i == 0)
    def _():
        z_scr[...] = jnp.zeros_like(z_scr)

    for d, (x_ref, hp_ref, hn_ref) in enumerate(((xf_ref, hpf_ref, hnf_ref), (xb_ref, hpb_ref, hnb_ref))):
        ib = i if d == 0 else nb - 1 - i
        prev_row = jnp.where(ib == 0, 0.0, hp_ref[0, HALO - 1:HALO, :])
        next_row = jnp.where(ib == nb - 1, 0.0, hn_ref[0, 0:1, :])
        r, k, v, z, kk = _rwkv_prep(x_ref[0], prev_row, next_row, mu_ref, kk_ref)
        logw, a_d, k_d = _rwkv_direction(d, z, jnp.tanh(z), k, w0_ref, ww2_ref, a0_ref, wa2_ref, ka_ref)
        r_scr[d] = r
        lw_scr[d] = logw
        k_scr[d] = k_d
        v_scr[d] = v
        a_scr[d] = -kk
        b_scr[d] = kk * a_d

    rowc = lax.broadcasted_iota(jnp.int32, (L, GROUP_W), 0)
    n4 = GROUP_HEADS * L
    tr = lax.broadcasted_iota(jnp.int32, (n4, n4), 0)
    tc = lax.broadcasted_iota(jnp.int32, (n4, n4), 1)
    trm = tr % L
    tcm = tc % L
    eye = (tr == tc).astype(F32)
    hid_l = _head_id((L, GROUP_W), 1)

    def stack(t):
        return jnp.concatenate([jnp.where(hid_l == h, t, 0.0) for h in range(GROUP_HEADS)], axis=0)

    def operands(d, cc):
        rev = d == 1
        strict = (tcm > trm) if rev else (tcm < trm)
        incl = (tcm >= trm) if rev else (tcm <= trm)
        rows = pl.ds(pl.multiple_of(cc * L, L), L)
        lw = lw_scr[d, rows, :]
        cs = lw
        s = 1
        while s < L:
            shifted = pltpu.roll(cs, (L - s) if rev else s, axis=0)
            cs = cs + jnp.where((rowc < L - s) if rev else (rowc >= s), shifted, 0.0)
            s *= 2
        tot = cs[0:1, :] if rev else cs[L - 1:L, :]
        ep = jnp.exp(cs)
        en = jnp.exp(-cs)
        ee = jnp.exp(tot - cs)
        kc = k_scr[d, rows, :]
        bc = b_scr[d, rows, :]
        a_s = stack(a_scr[d, rows, :] * jnp.exp(cs - lw))
        b_s = stack(bc * en)
        k_s = stack(kc * en)
        r_s = stack(r_scr[d, rows, :] * ep)
        o = dict(rows=rows, a_s=a_s, r_s=r_s, v_s=stack(v_scr[d, rows, :]), be_s=stack(bc * ee),
                 ke_s=stack(kc * ee), p_tot=jnp.exp(tot))
        o["aab"] = jnp.where(strict, _mm_nt(a_s, b_s), 0.0)
        o["aak"] = jnp.where(strict, _mm_nt(a_s, k_s), 0.0)
        o["arb"] = jnp.where(incl, _mm_nt(r_s, b_s), 0.0)
        o["ark"] = jnp.where(incl, _mm_nt(r_s, k_s), 0.0)
        return o

    dirs = (0, 1)
    y_refs = (yf_ref, yb_ref)

    def both(c, carry):
        o = [operands(0, c), operands(1, nch - 1 - c)]
        inv = [eye + o[d]["aab"] for d in dirs]
        pw = [o[d]["aab"] for d in dirs]
        for _ in range(5):
            pw = [_mm(pw[d], pw[d]) for d in dirs]
            inv = [inv[d] + _mm(inv[d], pw[d]) for d in dirs]
        s0 = [z_scr[d] for d in dirs]
        x = [_mm_nt(o[d]["a_s"], s0[d]) + _mm(o[d]["aak"], o[d]["v_s"]) for d in dirs]
        u = [_mm(inv[d], x[d]) for d in dirs]
        y_s = [_mm_nt(o[d]["r_s"], s0[d]) + _mm(o[d]["arb"], u[d]) + _mm(o[d]["ark"], o[d]["v_s"]) for d in dirs]
        for d in dirs:
            y_refs[d][0, o[d]["rows"], :] = (y_s[d][0:L] + y_s[d][L:2 * L] + y_s[d][2 * L:3 * L]
                                             + y_s[d][3 * L:4 * L])
            z_scr[d] = o[d]["p_tot"] * s0[d] + _mm_tn(u[d], o[d]["be_s"]) + _mm_tn(o[d]["v_s"], o[d]["ke_s"])
        return carry

    lax.fori_loop(0, nch, both, 0)


def _rwkv_final_kernel(x_ref, hp_ref, hn_ref, mu_ref, w0_ref, ww2_ref, a0_ref, wa2_ref, wg2_ref, kk_ref, ka_ref,
                       rk_ref, lng_ref, lnb_ref, yf_ref, yb_ref, out_ref):
    nb = pl.num_programs(1)
    ib = pl.program_id(1)
    prev_row = jnp.where(ib == 0, 0.0, hp_ref[0, HALO - 1:HALO, :])
    next_row = jnp.where(ib == nb - 1, 0.0, hn_ref[0, 0:1, :])
    r, k, v, z, _ = _rwkv_prep(x_ref[0], prev_row, next_row, mu_ref, kk_ref)
    zt = jnp.tanh(z)
    _, _, k_0 = _rwkv_direction(0, z, zt, k, w0_ref, ww2_ref, a0_ref, wa2_ref, ka_ref)
    _, _, k_1 = _rwkv_direction(1, z, zt, k, w0_ref, ww2_ref, a0_ref, wa2_ref, ka_ref)
    hid = _head_id(r.shape, 1)
    y = yf_ref[0] + yb_ref[0]
    mean = _head_sum_bcast(y, hid) * (1.0 / HEAD_DIM)
    yc = y - mean
    var = _head_sum_bcast(yc * yc, hid) * (1.0 / HEAD_DIM)
    y = yc * lax.rsqrt(var + RWKV_GN_EPS) * lng_ref[...] + lnb_ref[...]
    bonus = _head_sum_bcast(r * (k_0 + k_1) * rk_ref[...], hid) * v
    gate = _mm(_sigmoid(z), wg2_ref[...])
    out_ref[0] = ((y + bonus) * gate).astype(out_ref.dtype)


def rwkv_mixer(pr, mu, w0, w_w2, a0, w_a2, w_g2, k_k, k_a, r_k, ln_g, ln_b, tb=512):
    S, T, _ = pr.shape
    tb = min(tb, T)
    nb = T // tb
    hb = tb // HALO
    nh = T // HALO

    def lora_pad(w, lo):
        r = w.shape[-2]
        pad = [(0, 0)] * (w.ndim - 2) + [(lo, LANES - lo - r), (0, 0)]
        return jnp.pad(w, pad).astype(BF16)

    ww2 = lora_pad(w_w2, 0)
    wa2 = lora_pad(w_a2, 32)
    wg2 = lora_pad(w_g2, 64)
    row = lambda t: t.reshape(1, GROUP_W)
    full = lambda t: pl.BlockSpec(t.shape, lambda s, i: (0,) * t.ndim)

    def specs(blk):
        return [pl.BlockSpec((1, tb, RWKV_IN), lambda s, i: (s, blk(i), 0)),
                pl.BlockSpec((1, HALO, RWKV_IN), lambda s, i: (s, jnp.maximum(blk(i) * hb - 1, 0), 0)),
                pl.BlockSpec((1, HALO, RWKV_IN), lambda s, i: (s, jnp.minimum((blk(i) + 1) * hb, nh - 1), 0))]

    fwd = lambda i: i
    bwd = lambda i: nb - 1 - i
    params = (mu, w0, ww2, a0, wa2, row(k_k), row(k_a))
    y_shape = jax.ShapeDtypeStruct((S, T, GROUP_W), F32)
    y_fw, y_bw = pl.pallas_call(
        functools.partial(_rwkv_scan_kernel, tb),
        grid=(S, nb),
        in_specs=specs(fwd) + specs(bwd) + [full(t) for t in params],
        out_specs=[pl.BlockSpec((1, tb, GROUP_W), lambda s, i: (s, i, 0)),
                   pl.BlockSpec((1, tb, GROUP_W), lambda s, i: (s, nb - 1 - i, 0))],
        out_shape=[y_shape, y_shape],
        scratch_shapes=[pltpu.VMEM((2, GROUP_W, GROUP_W), F32)]
        + [pltpu.VMEM((2, tb, GROUP_W), F32) for _ in range(6)],
        compiler_params=_cparams(("parallel", "arbitrary")),
        name="rwkv_scan",
    )(pr, pr, pr, pr, pr, pr, *params)
    fparams = (mu, w0, ww2, a0, wa2, wg2, row(k_k), row(k_a), row(r_k), row(ln_g), row(ln_b))
    yblk = pl.BlockSpec((1, tb, GROUP_W), lambda s, i: (s, i, 0))
    return pl.pallas_call(
        _rwkv_final_kernel,
        grid=(S, nb),
        in_specs=specs(fwd) + [full(t) for t in fparams] + [yblk, yblk],
        out_specs=yblk,
        out_shape=jax.ShapeDtypeStruct((S, T, GROUP_W), BF16),
        compiler_params=_cparams(("parallel", "parallel")),
        name="rwkv_final",
    )(pr, pr, pr, *fparams, y_fw, y_bw)


def _gelu_tanh(x):
    return 0.5 * x * (1.0 + jnp.tanh(math.sqrt(2.0 / math.pi) * (x + 0.044715 * (x * x * x))))


def _lru_kernel(rev, tb, *refs):
    if rev:
        x_ref, hp_ref, hn_ref, cw_ref, cb_ref, gw_ref, gb_ref, lam_ref, hfw_ref, out_ref, h_scr = refs
    else:
        x_ref, hp_ref, hn_ref, cw_ref, cb_ref, gw_ref, gb_ref, lam_ref, out_ref, h_scr = refs
    nb = pl.num_programs(1)
    ib = (nb - 1 - pl.program_id(1)) if rev else pl.program_id(1)

    @pl.when(pl.program_id(1) == 0)
    def _():
        h_scr[...] = jnp.zeros_like(h_scr)

    xb = x_ref[0, :, 0:GROUP_W]
    first = ib == 0
    last = ib == nb - 1
    pm2 = jnp.where(first, 0.0, hp_ref[0, HALO - 2:HALO - 1, 0:GROUP_W])
    pm1 = jnp.where(first, 0.0, hp_ref[0, HALO - 1:HALO, 0:GROUP_W])
    nx1 = jnp.where(last, 0.0, hn_ref[0, 0:1, 0:GROUP_W])
    rowi = lax.broadcasted_iota(jnp.int32, (tb, GROUP_W), 0)
    x_m1 = jnp.where(rowi == 0, pm1, pltpu.roll(xb, 1, axis=0))
    x_m2 = jnp.where(rowi == 0, pm2, jnp.where(rowi == 1, pm1, pltpu.roll(xb, 2, axis=0)))
    x_p1 = jnp.where(rowi == tb - 1, nx1, pltpu.roll(xb, tb - 1, axis=0))
    xc = (cb_ref[...] + x_m2 * cw_ref[0:1, :] + x_m1 * cw_ref[1:2, :] + xb * cw_ref[2:3, :]
          + x_p1 * cw_ref[3:4, :])
    xcb = xc.astype(BF16)
    r_gate = _sigmoid(_dot(xcb, gw_ref[0]) + gb_ref[0:1, :])
    i_gate = _sigmoid(_dot(xcb, gw_ref[1]) + gb_ref[1:2, :])
    lam = lam_ref[...]
    softplus_neg_lam = jnp.maximum(-lam, 0.0) + jnp.log(1.0 + jnp.exp(-jnp.abs(lam)))
    log_a = -LRU_C * r_gate * softplus_neg_lam
    a = jnp.exp(log_a)
    u = jnp.sqrt(1.0 - jnp.exp(2.0 * log_a)) * (i_gate * xc)

    s = 1
    while s < tb:
        shift = (tb - s) if rev else s
        valid = (rowi < tb - s) if rev else (rowi >= s)
        a_sh = pltpu.roll(a, shift, axis=0)
        u_sh = pltpu.roll(u, shift, axis=0)
        u = jnp.where(valid, a * u_sh + u, u)
        a = jnp.where(valid, a * a_sh, a)
        s *= 2
    h = a * h_scr[0:1, :] + u
    h_scr[0:1, :] = h[0:1, :] if rev else h[tb - 1:tb, :]

    if rev:
        out_ref[0] = ((hfw_ref[0] + h) * _gelu_tanh(x_ref[0, :, GROUP_W:2 * GROUP_W])).astype(out_ref.dtype)
    else:
        out_ref[0] = h


def lru_mixer(pl_in, conv_w, conv_b, gate_w, gate_b, lam, tb=512):
    S, T, _ = pl_in.shape
    tb = min(tb, T)
    nb = T // tb
    hb = tb // HALO
    nh = T // HALO
    eye = jnp.eye(GROUP_HEADS, dtype=F32)
    gw = jnp.einsum("dgnij,nm->dgnimj", gate_w, eye).reshape(2, 2, GROUP_W, GROUP_W).astype(BF16)
    full = lambda t: pl.BlockSpec(t.shape, lambda s, i: (0,) * t.ndim)

    def specs(rev):
        blk = (lambda i: nb - 1 - i) if rev else (lambda i: i)
        return [pl.BlockSpec((1, tb, LRU_IN), lambda s, i: (s, blk(i), 0)),
                pl.BlockSpec((1, HALO, LRU_IN), lambda s, i: (s, jnp.maximum(blk(i) * hb - 1, 0), 0)),
                pl.BlockSpec((1, HALO, LRU_IN), lambda s, i: (s, jnp.minimum((blk(i) + 1) * hb, nh - 1), 0))]

    def args(d):
        return (conv_w, conv_b.reshape(1, GROUP_W), gw[d], gate_b[d], lam[d].reshape(1, GROUP_W))

    scratch = [pltpu.VMEM((SUBLANES, GROUP_W), F32)]
    h_fw = pl.pallas_call(
        functools.partial(_lru_kernel, False, tb),
        grid=(S, nb),
        in_specs=specs(False) + [full(t) for t in args(0)],
        out_specs=pl.BlockSpec((1, tb, GROUP_W), lambda s, i: (s, i, 0)),
        out_shape=jax.ShapeDtypeStruct((S, T, GROUP_W), F32),
        scratch_shapes=scratch,
        compiler_params=_cparams(("parallel", "arbitrary")),
        name="lru_fw",
    )(pl_in, pl_in, pl_in, *args(0))
    return pl.pallas_call(
        functools.partial(_lru_kernel, True, tb),
        grid=(S, nb),
        in_specs=specs(True) + [full(t) for t in args(1)]
        + [pl.BlockSpec((1, tb, GROUP_W), lambda s, i: (s, nb - 1 - i, 0))],
        out_specs=pl.BlockSpec((1, tb, GROUP_W), lambda s, i: (s, nb - 1 - i, 0)),
        out_shape=jax.ShapeDtypeStruct((S, T, GROUP_W), BF16),
        scratch_shapes=scratch,
        compiler_params=_cparams(("parallel", "arbitrary")),
        name="lru_bw",
    )(pl_in, pl_in, pl_in, *args(1), h_fw)


def _out_proj_kernel(x_ref, ym_ref, ya_ref, yr_ref, yl_ref, w_ref, o_ref):
    acc = x_ref[...]
    for i, y_ref in enumerate((ym_ref, ya_ref, yr_ref, yl_ref)):
        acc = acc + _dot(y_ref[...], w_ref[i * GROUP_W:(i + 1) * GROUP_W, :])
    o_ref[...] = acc


def out_proj(x, ys, w):
    n = x.shape[0]
    tm = min(512, n)
    return pl.pallas_call(
        _out_proj_kernel,
        grid=(n // tm,),
        in_specs=[pl.BlockSpec((tm, D_MODEL), lambda i: (i, 0))]
        + [pl.BlockSpec((tm, GROUP_W), lambda i: (i, 0)) for _ in ys]
        + [pl.BlockSpec((D_MODEL, D_MODEL), lambda i: (0, 0))],
        out_specs=pl.BlockSpec((tm, D_MODEL), lambda i: (i, 0)),
        out_shape=jax.ShapeDtypeStruct((n, D_MODEL), F32),
        compiler_params=_cparams(("parallel",)),
        name="out_proj",
    )(x, *ys, w)


XA_W = XA_HEADS * XA_DIM


def _mem_kv_kernel(m_ref, g_ref, wkv_ref, kg_ref, k_ref, v_ref):
    hm = (_rms(m_ref[0], D_MODEL) * g_ref[...]).astype(BF16)
    kv = _dot(hm, wkv_ref[...])
    k = kv[:, 0:XA_W]
    hid = _head_id(k.shape, 1, XA_DIM)
    k = k * lax.rsqrt(_head_sum_bcast(k * k, hid, XA_HEADS) * (1.0 / XA_DIM) + NORM_EPS) * kg_ref[...]
    k_ref[0] = k.astype(k_ref.dtype)
    v_ref[0] = kv[:, XA_W:2 * XA_W].astype(v_ref.dtype)


def mem_kv(mem, g, wkv, k_g):
    S, M, D = mem.shape
    full = lambda t: pl.BlockSpec(t.shape, lambda s: (0,) * t.ndim)
    args = (g.reshape(1, D), wkv.astype(BF16), jnp.tile(k_g, XA_HEADS).reshape(1, XA_W))
    return pl.pallas_call(
        _mem_kv_kernel,
        grid=(S,),
        in_specs=[pl.BlockSpec((1, M, D), lambda s: (s, 0, 0))] + [full(t) for t in args],
        out_specs=[pl.BlockSpec((1, M, XA_W), lambda s: (s, 0, 0))] * 2,
        out_shape=[jax.ShapeDtypeStruct((S, M, XA_W), BF16)] * 2,
        compiler_params=_cparams(("parallel",)),
        name="mem_kv",
    )(mem, *args)


def _xattn_kernel(x_ref, g_ref, wq_ref, qg_ref, k_ref, v_ref, wo_ref, gf_ref, rt_ref, x2_ref, aff_ref):
    x = x_ref[0]
    h = (_rms(x, D_MODEL) * g_ref[...]).astype(BF16)
    q = _dot(h, wq_ref[...])
    hid = _head_id(q.shape, 1, XA_DIM)
    q = q * lax.rsqrt(_head_sum_bcast(q * q, hid, XA_HEADS) * (1.0 / XA_DIM) + NORM_EPS) * qg_ref[...]
    q = q * (XA_DIM ** -0.5)
    k = k_ref[0]
    v = v_ref[0]
    o = jnp.zeros(q.shape, F32)
    for hh in range(XA_HEADS):
        hm = hid == hh
        s = _dot_nt(jnp.where(hm, q, 0.0).astype(BF16), k)
        p = jnp.exp(s - jnp.max(s, axis=-1, keepdims=True))
        oh = _dot(p.astype(BF16), v) / jnp.sum(p, axis=-1, keepdims=True)
        o = jnp.where(hm, oh, o)
    x2 = x + _dot(o.astype(BF16), wo_ref[...])
    x2_ref[0] = x2
    h3 = _rms(x2, D_MODEL) * gf_ref[...]
    h_hi = h3.astype(BF16)
    h_lo = (h3 - h_hi.astype(F32)).astype(BF16)
    hi_part = _dot_nt(rt_ref[...], h_hi)
    logits = hi_part[0:N_EXPERTS] + hi_part[N_EXPERTS:2 * N_EXPERTS] + _dot_nt(rt_ref[0:N_EXPERTS, :], h_lo)
    e = jnp.exp(logits - jnp.max(logits, axis=0, keepdims=True))
    aff_ref[...] = e / jnp.sum(e, axis=0, keepdims=True)


def xattn_router(x, g_xa, wq, q_g, k, v, wo, g_ffn, router):
    S, T, D = x.shape
    tm = min(256, T)
    nb = T // tm
    args = (g_xa.reshape(1, D), wq.astype(BF16), jnp.tile(q_g, XA_HEADS).reshape(1, XA_W))
    rt = router.T
    rt_hi = rt.astype(BF16)
    rt_lo = (rt - rt_hi.astype(F32)).astype(BF16)
    args2 = (wo.astype(BF16), g_ffn.reshape(1, D), jnp.concatenate([rt_hi, rt_lo], axis=0))
    full = lambda t: pl.BlockSpec(t.shape, lambda s, i: (0,) * t.ndim)
    M = k.shape[1]
    return pl.pallas_call(
        _xattn_kernel,
        grid=(S, nb),
        in_specs=[pl.BlockSpec((1, tm, D), lambda s, i: (s, i, 0))] + [full(t) for t in args]
        + [pl.BlockSpec((1, M, XA_W), lambda s, i: (s, 0, 0))] * 2 + [full(t) for t in args2],
        out_specs=[pl.BlockSpec((1, tm, D), lambda s, i: (s, i, 0)),
                   pl.BlockSpec((N_EXPERTS, tm), lambda s, i: (0, s * nb + i))],
        out_shape=[jax.ShapeDtypeStruct((S, T, D), F32), jax.ShapeDtypeStruct((N_EXPERTS, S * T), F32)],
        compiler_params=_cparams(("parallel", "parallel")),
        name="xattn_router",
    )(x, *args, k, v, *args2)


def _excl_prefix(m):
    nb = m.shape[0]
    li = lax.broadcasted_iota(jnp.int32, (LANES, LANES), 0)
    lj = lax.broadcasted_iota(jnp.int32, (LANES, LANES), 1)
    within = _dot(m.astype(BF16), (li < lj).astype(BF16))
    tot = jnp.broadcast_to(jnp.sum(m, axis=-1, keepdims=True), (nb, LANES))
    bi = lax.broadcasted_iota(jnp.int32, (nb, nb), 0)
    bj = lax.broadcasted_iota(jnp.int32, (nb, nb), 1)
    carry = _dot((bj < bi).astype(F32), tot, HI)
    carry_row = _dot_tn(tot[:, 0:SUBLANES], (bi < bj).astype(F32), HI)
    return within + carry, carry_row


def _route_select_kernel(cap, aff_ref, sel_ref, pos_ref, carry_ref):
    a = aff_ref[0]
    bits = pltpu.bitcast(a, jnp.int32)

    def bisect(i, prefix):
        cand = prefix | jnp.left_shift(jnp.int32(1), 30 - i)
        ge = (bits >= cand).astype(F32)
        cnt = jnp.sum(jnp.sum(ge, axis=-1, keepdims=True), axis=0, keepdims=True)
        return jnp.where(cnt >= cap, cand, prefix)

    thr = lax.fori_loop(0, 31, bisect, jnp.zeros((1, 1), jnp.int32))
    gt = bits > thr
    eq = (bits == thr).astype(F32)
    n_gt = jnp.sum(jnp.sum(gt.astype(F32), axis=-1, keepdims=True), axis=0, keepdims=True)
    eq_rank, _ = _excl_prefix(eq)
    sel = jnp.where(gt, 1.0, jnp.where(eq_rank < cap - n_gt, eq, 0.0))
    pos, carry_row = _excl_prefix(sel)
    sel_ref[0] = sel
    pos_ref[0] = pos
    carry_ref[0] = carry_row.astype(jnp.int32)


def route_select(aff, cap):
    E, ng = aff.shape
    nb = ng // LANES
    blk = pl.BlockSpec((1, nb, LANES), lambda e: (e, 0, 0))
    sel, pos, carry = pl.pallas_call(
        functools.partial(_route_select_kernel, cap),
        grid=(E,),
        in_specs=[blk],
        out_specs=[blk, blk, pl.BlockSpec((1, SUBLANES, nb), lambda e: (e, 0, 0))],
        out_shape=[jax.ShapeDtypeStruct((E, nb, LANES), F32), jax.ShapeDtypeStruct((E, nb, LANES), F32),
                   jax.ShapeDtypeStruct((E, SUBLANES, nb), jnp.int32)],
        compiler_params=_cparams(("parallel",)),
        name="route_select",
    )(aff.reshape(E, nb, LANES))
    return sel, pos, carry[:, 0, :]


MOE_TILE = 256
MOE_STRIP = 64
MOE_BLOCK = 256
ROW_ALIGN = 16


def _round_up(x, m):
    return (x + m - 1) // m * m


def _lane_row(ref, e):
    return ref[e, 0]


def _moe_dispatch_kernel(flush_after, off_ref, nsub_ref, used_ref, x_ref, g_ref, sel_ref, pos_ref, aff_ref,
                         xe_hbm, ge_hbm, strip, gstrip, hbuf, obuf, ogbuf, zbuf, zgbuf, sem_x, sem_g, sem_o):
    t = pl.program_id(0)
    nt = pl.num_programs(0)
    slot = t % 2
    R = MOE_STRIP
    sub = lax.broadcasted_iota(jnp.int32, (R, MOE_TILE), 0).astype(F32)

    def strip_copies(tt, sl):
        cps = []
        for e in range(N_EXPERTS):
            row0 = pl.multiple_of(off_ref[e, tt], ROW_ALIGN)
            cps.append(pltpu.make_async_copy(strip.at[sl, pl.ds(e * R, R), :], xe_hbm.at[pl.ds(row0, R), :],
                                             sem_x.at[sl]))
            cps.append(pltpu.make_async_copy(gstrip.at[sl, pl.ds(e * R, R), :], ge_hbm.at[pl.ds(row0, R), :],
                                             sem_g.at[sl]))
        return cps

    def onehot_and_gate(e, s):
        sel = _lane_row(sel_ref, e)
        pos = _lane_row(pos_ref, e)
        lpos = pos - pos[:, 0:1] - jnp.asarray(s * R, F32)
        oh = jnp.where(lpos == sub, sel, 0.0)
        gate = jnp.sum(oh * _lane_row(aff_ref, e), axis=-1, keepdims=True)
        return oh.astype(BF16), jnp.broadcast_to(gate, (R, LANES))

    hbuf[...] = (_rms(x_ref[...], D_MODEL) * g_ref[...]).astype(BF16)
    parts = [onehot_and_gate(e, 0) for e in range(N_EXPERTS)]
    strip[slot] = _dot(jnp.concatenate([p[0] for p in parts], axis=0), hbuf[...]).astype(BF16)
    gstrip[slot] = jnp.concatenate([p[1] for p in parts], axis=0)

    @pl.when(t > 0)
    def _():
        for cp in strip_copies(t - 1, 1 - slot):
            cp.wait()

    def flush(group):
        _, g, row_base, cpad = group
        zbuf[...] = jnp.zeros_like(zbuf)
        zgbuf[...] = jnp.zeros_like(zgbuf)

        def zero_copies(row0):
            row0 = pl.multiple_of(row0, ROW_ALIGN)
            return (pltpu.make_async_copy(zbuf, xe_hbm.at[pl.ds(row0, MOE_BLOCK), :], sem_o.at[0]),
                    pltpu.make_async_copy(zgbuf, ge_hbm.at[pl.ds(row0, MOE_BLOCK), :], sem_o.at[1]))

        heads = [cp for e in range(N_EXPERTS) for cp in zero_copies(used_ref[e, g])]
        for cp in heads:
            cp.start()
        for cp in heads:
            cp.wait()
        for wait in (False, True):
            for e in range(N_EXPERTS):
                first_block = (used_ref[e, g] + MOE_BLOCK - 1) // MOE_BLOCK * MOE_BLOCK
                nblocks = (row_base + (e + 1) * cpad - first_block) // MOE_BLOCK

                def block(i, carry):
                    for cp in zero_copies(first_block + i * MOE_BLOCK):
                        cp.wait() if wait else cp.start()
                    return carry
                lax.fori_loop(0, nblocks, block, 0)

    for group in flush_after[:-1]:
        @pl.when(t == group[0] + 1)
        def _():
            flush(group)

    for cp in strip_copies(t, slot):
        cp.start()

    for e in range(N_EXPERTS):
        def extra(s, carry):
            oh, gate = onehot_and_gate(e, s)
            obuf[...] = _dot(oh, hbuf[...]).astype(BF16)
            ogbuf[...] = gate
            row0 = pl.multiple_of(off_ref[e, t] + s * R, ROW_ALIGN)
            cx = pltpu.make_async_copy(obuf, xe_hbm.at[pl.ds(row0, R), :], sem_o.at[0])
            cg = pltpu.make_async_copy(ogbuf, ge_hbm.at[pl.ds(row0, R), :], sem_o.at[1])
            cx.start()
            cg.start()
            cx.wait()
            cg.wait()
            return carry
        lax.fori_loop(1, nsub_ref[e, t], extra, 0)

    @pl.when(t == nt - 1)
    def _():
        for cp in strip_copies(t, slot):
            cp.wait()
        flush(flush_after[-1])


def _moe_ffn_kernel(blk_ref, valid_ref, xe_ref, ge_ref, wg_ref, wu_ref, wd_ref, y_ref):
    del blk_ref
    valid = valid_ref[pl.program_id(0), pl.program_id(1)] == 1

    @pl.when(valid)
    def _():
        h = xe_ref[...]
        gate_proj = _dot(h, wg_ref[0])
        hid = gate_proj * _sigmoid(gate_proj) * _dot(h, wu_ref[0])
        y_ref[...] = (_dot(hid.astype(BF16), wd_ref[0]) * ge_ref[:, 0:1]).astype(y_ref.dtype)

    @pl.when(jnp.logical_not(valid))
    def _():
        y_ref[...] = jnp.zeros_like(y_ref)


def _moe_combine_kernel(off_ref, nsub_ref, x_ref, selt_ref, post_ref, y_hbm, o_ref, ybuf, obuf, sem_y, sem_o):
    t = pl.program_id(0)
    nt = pl.num_programs(0)
    slot = t % 2
    R = MOE_STRIP

    def fetch(tt, sl):
        return [pltpu.make_async_copy(y_hbm.at[pl.ds(pl.multiple_of(off_ref[e, tt], ROW_ALIGN), R), :],
                                      ybuf.at[sl, pl.ds(e * R, R), :], sem_y.at[sl])
                for e in range(N_EXPERTS)]

    @pl.when(t == 0)
    def _():
        for cp in fetch(0, 0):
            cp.start()

    @pl.when(t + 1 < nt)
    def _():
        for cp in fetch(t + 1, 1 - slot):
            cp.start()

    lane = lax.broadcasted_iota(jnp.int32, (MOE_TILE, LANES), 1).astype(F32)

    def local_pos(e):
        return post_ref[:, e:e + 1] - post_ref[0:1, e:e + 1]

    tiles = []
    for i in range(N_EXPERTS // 2):
        e0, e1 = 2 * i, 2 * i + 1
        w0 = jnp.where(local_pos(e0) == lane, selt_ref[:, e0:e0 + 1], 0.0)
        w1 = jnp.where(local_pos(e1) == lane - R, selt_ref[:, e1:e1 + 1], 0.0)
        tiles.append(jnp.where(lane < R, w0, w1).astype(BF16))
    w = jnp.concatenate(tiles, axis=1)

    for cp in fetch(t, slot):
        cp.wait()
    o_ref[...] = x_ref[...] + _dot(w, ybuf[slot])

    for e in range(N_EXPERTS):
        def extra(s, carry):
            obuf[R:2 * R, :] = jnp.zeros((R, D_MODEL), obuf.dtype)
            row0 = pl.multiple_of(off_ref[e, t] + s * R, ROW_ALIGN)
            cp = pltpu.make_async_copy(y_hbm.at[pl.ds(row0, R), :], obuf.at[pl.ds(0, R), :], sem_o)
            cp.start()
            cp.wait()
            ws = jnp.where(local_pos(e) - jnp.asarray(s * R, F32) == lane, selt_ref[:, e:e + 1], 0.0)
            o_ref[...] += _dot(ws.astype(BF16), obuf[...])
            return carry
        lax.fori_loop(1, nsub_ref[e, t], extra, 0)


def moe_layer(x2, aff, groups, g_ffn, w_gate, w_up, w_down):
    n, D = x2.shape
    E = N_EXPERTS
    ff = w_gate.shape[-1]
    nt_total = n // MOE_TILE
    sels, poss, offs, nsubs, useds, blks, valids, flush_after = [], [], [], [], [], [], [], []
    tok0, tile0, row_base = 0, 0, 0
    for g, (ng, cap) in enumerate(groups):
        sel, pos, carry = route_select(aff[:, tok0:tok0 + ng], cap)
        sels.append(sel)
        poss.append(pos)
        ntile = ng // MOE_TILE
        cpad = _round_up(cap + ROW_ALIGN * ntile + MOE_STRIP, MOE_BLOCK) + MOE_BLOCK
        start = carry[:, ::MOE_TILE // LANES]
        cnt = jnp.diff(start, axis=1, append=jnp.full((E, 1), cap, jnp.int32))
        padded = _round_up(cnt, ROW_ALIGN)
        region = row_base + cpad * jnp.arange(E, dtype=jnp.int32)[:, None]
        off = jnp.cumsum(padded, axis=1) - padded
        used = jnp.sum(padded, axis=1, keepdims=True)
        offs.append(region + off)
        nsubs.append(_round_up(cnt, MOE_STRIP) // MOE_STRIP)
        useds.append(region + used)
        nblk = cpad // MOE_BLOCK
        last = (used + MOE_STRIP - 1) // MOE_BLOCK
        c = jnp.arange(nblk, dtype=jnp.int32)[None, :]
        blks.append(region // MOE_BLOCK + c)
        valids.append((c <= last).astype(jnp.int32))
        flush_after.append((tile0 + ntile - 1, g, row_base, cpad))
        tok0 += ng
        tile0 += ntile
        row_base += E * cpad
    sel = jnp.concatenate(sels, axis=1)
    pos = jnp.concatenate(poss, axis=1)
    off = jnp.concatenate(offs, axis=1).astype(jnp.int32)
    nsub = jnp.concatenate(nsubs, axis=1).astype(jnp.int32)
    used = jnp.concatenate(useds, axis=1).astype(jnp.int32)
    blk = jnp.concatenate(blks, axis=1).astype(jnp.int32)
    valid = jnp.concatenate(valids, axis=1)
    rows_total = row_base
    any_spec = pl.BlockSpec(memory_space=pl.ANY)
    tile_rows = pl.BlockSpec((E, 1, 1, MOE_TILE), lambda t, *_: (0, t, 0, 0))
    by_tile = lambda a: a.reshape(E, nt_total, 1, MOE_TILE)

    xe, ge = pl.pallas_call(
        functools.partial(_moe_dispatch_kernel, tuple(flush_after)),
        grid_spec=pltpu.PrefetchScalarGridSpec(
            num_scalar_prefetch=3, grid=(nt_total,),
            in_specs=[pl.BlockSpec((MOE_TILE, D), lambda t, *_: (t, 0)),
                      pl.BlockSpec((1, D), lambda t, *_: (0, 0)),
                      tile_rows, tile_rows, tile_rows],
            out_specs=[any_spec, any_spec],
            scratch_shapes=[pltpu.VMEM((2, E * MOE_STRIP, D), BF16), pltpu.VMEM((2, E * MOE_STRIP, LANES), F32),
                            pltpu.VMEM((MOE_TILE, D), BF16),
                            pltpu.VMEM((MOE_STRIP, D), BF16), pltpu.VMEM((MOE_STRIP, LANES), F32),
                            pltpu.VMEM((MOE_BLOCK, D), BF16), pltpu.VMEM((MOE_BLOCK, LANES), F32),
                            pltpu.SemaphoreType.DMA((2,)), pltpu.SemaphoreType.DMA((2,)),
                            pltpu.SemaphoreType.DMA((2,))]),
        out_shape=[jax.ShapeDtypeStruct((rows_total, D), BF16), jax.ShapeDtypeStruct((rows_total, LANES), F32)],
        compiler_params=_cparams(("arbitrary",)),
        name="moe_dispatch",
    )(off, nsub, used, x2, g_ffn.reshape(1, D), by_tile(sel), by_tile(pos), by_tile(aff))

    nblk_total = blk.shape[1]
    y = pl.pallas_call(
        _moe_ffn_kernel,
        grid_spec=pltpu.PrefetchScalarGridSpec(
            num_scalar_prefetch=2, grid=(E, nblk_total),
            in_specs=[pl.BlockSpec((MOE_BLOCK, D), lambda e, c, blk, valid: (blk[e, c], 0)),
                      pl.BlockSpec((MOE_BLOCK, LANES), lambda e, c, blk, valid: (blk[e, c], 0)),
                      pl.BlockSpec((1, D, ff), lambda e, c, *_: (e, 0, 0)),
                      pl.BlockSpec((1, D, ff), lambda e, c, *_: (e, 0, 0)),
                      pl.BlockSpec((1, ff, D), lambda e, c, *_: (e, 0, 0))],
            out_specs=pl.BlockSpec((MOE_BLOCK, D), lambda e, c, blk, valid: (blk[e, c], 0))),
        out_shape=jax.ShapeDtypeStruct((rows_total, D), BF16),
        compiler_params=_cparams(("arbitrary", "arbitrary")),
        name="moe_ffn",
    )(blk, valid, xe, ge, w_gate, w_up, w_down)

    sel_t = sel.reshape(E, n).T
    pos_t = pos.reshape(E, n).T
    return pl.pallas_call(
        _moe_combine_kernel,
        grid_spec=pltpu.PrefetchScalarGridSpec(
            num_scalar_prefetch=2, grid=(nt_total,),
            in_specs=[pl.BlockSpec((MOE_TILE, D), lambda t, *_: (t, 0)),
                      pl.BlockSpec((MOE_TILE, E), lambda t, *_: (t, 0)),
                      pl.BlockSpec((MOE_TILE, E), lambda t, *_: (t, 0)),
                      any_spec],
            out_specs=pl.BlockSpec((MOE_TILE, D), lambda t, *_: (t, 0)),
            scratch_shapes=[pltpu.VMEM((2, E * MOE_STRIP, D), BF16), pltpu.VMEM((2 * MOE_STRIP, D), BF16),
                            pltpu.SemaphoreType.DMA((2,)), pltpu.SemaphoreType.DMA]),
        out_shape=jax.ShapeDtypeStruct((n, D), F32),
        compiler_params=_cparams(("arbitrary",)),
        name="moe_combine",
    )(off, nsub, x2, sel_t, pos_t, y)


def kernel(x_prompt, x_sample, mem_prompt, mem_sample, norm_mix_g, w_in, mlstm_gate_b, mlstm_head_g, mla_qa_g, mla_w_uq, mla_kva_g, mla_w_ukv, mla_q_g, mla_k_g, rwkv_mu, rwkv_w0, rwkv_w_w2, rwkv_a0, rwkv_w_a2, rwkv_w_g2, rwkv_k_k, rwkv_k_a, rwkv_r_k, rwkv_ln_g, rwkv_ln_b, lru_conv_w, lru_conv_b, lru_gate_w, lru_gate_b, lru_lambda, w_out, norm_xa_g, norm_mem_g, xa_wq, xa_wkv, xa_q_g, xa_k_g, xa_wo, norm_ffn_g, moe_router, moe_w_gate, moe_w_up, moe_w_down):
    assert x_prompt.shape[1] == x_sample.shape[1]
    n_prompt = x_prompt.shape[0]
    x = jnp.concatenate([x_prompt, x_sample], axis=0)
    mem = jnp.concatenate([mem_prompt, mem_sample], axis=0)
    S, T, D = x.shape
    N = S * T
    tok_prompt = n_prompt * T
    tok_sample = N - tok_prompt
    cap_prompt = max(1, (CAPACITY_FACTOR * tok_prompt) // N_EXPERTS)
    cap_sample = max(1, (CAPACITY_FACTOR * tok_sample) // N_EXPERTS)
    w_in_p = prep_w_in(w_in)
    w_out_b = w_out.astype(BF16)
    w_gate_b = moe_w_gate.astype(BF16)
    w_up_b = moe_w_up.astype(BF16)
    w_down_b = moe_w_down.astype(BF16)
    x = x.reshape(N, D)
    for l in range(w_in.shape[0]):
        pm, pa, pr, pl_ = in_proj(x, norm_mix_g[l].reshape(1, D), w_in_p[l])
        y_m = mlstm_mixer(pm.reshape(S, T, -1), mlstm_gate_b[l], mlstm_head_g[l])
        y_a = mla_mixer(pa.reshape(S, T, -1), mla_qa_g[l], mla_w_uq[l], mla_kva_g[l], mla_w_ukv[l], mla_q_g[l],
                        mla_k_g[l])
        y_r = rwkv_mixer(pr.reshape(S, T, -1), rwkv_mu[l], rwkv_w0[l], rwkv_w_w2[l], rwkv_a0[l], rwkv_w_a2[l],
                         rwkv_w_g2[l], rwkv_k_k[l], rwkv_k_a[l], rwkv_r_k[l], rwkv_ln_g[l], rwkv_ln_b[l])
        y_l = lru_mixer(pl_.reshape(S, T, -1), lru_conv_w[l], lru_conv_b[l], lru_gate_w[l], lru_gate_b[l],
                        lru_lambda[l])
        x1 = out_proj(x, [y.reshape(N, GROUP_W) for y in (y_m, y_a, y_r, y_l)], w_out_b[l])
        k_mem, v_mem = mem_kv(mem, norm_mem_g[l], xa_wkv[l], xa_k_g[l])
        x2, aff = xattn_router(x1.reshape(S, T, D), norm_xa_g[l], xa_wq[l], xa_q_g[l], k_mem, v_mem,
                               xa_wo[l], norm_ffn_g[l], moe_router[l])
        x = moe_layer(x2.reshape(N, D), aff, [(tok_prompt, cap_prompt), (tok_sample, cap_sample)],
                      norm_ffn_g[l], w_gate_b[l], w_up_b[l], w_down_b[l])
    x = x.reshape(S, T, D)
    return x[:n_prompt], x[n_prompt:]
```

```python
import functools
import math

import jax
import jax.numpy as jnp
from jax import lax
from jax.experimental import pallas as pl
from jax.experimental.pallas import tpu as pltpu

F32 = jnp.float32
BF16 = jnp.bfloat16
HI = lax.Precision.HIGHEST

D_MODEL = 1024
NORM_EPS = 1e-6
GROUP_W = 256
HEAD_DIM = 64
GROUP_HEADS = 4

MLSTM_CHUNK = 128
MLSTM_IN = 4 * GROUP_W + 4 * GROUP_HEADS
MLSTM_PAD = 1152

MLA_Q_LORA = 192
MLA_KV_LORA = 128
MLA_NOPE = 64
MLA_ROPE = 32
MLA_QK = MLA_NOPE + MLA_ROPE
MLA_IN = MLA_Q_LORA + MLA_KV_LORA + MLA_ROPE
MLA_PAD = 512
ROPE_THETA = 10000.0
LOG2E = math.log2(math.e)

RWKV_IN = 3 * GROUP_W + 32 + 32 + 64
RWKV_GN_EPS = 64e-5
RWKV_CHUNK = 64

LRU_CONV = 4
LRU_C = 8.0
LRU_IN = 2 * GROUP_W

N_EXPERTS = 16
CAPACITY_FACTOR = 2
XA_HEADS = 4
XA_DIM = 64

VMEM_LIMIT_BYTES = 56 * 1024 * 1024
SUBLANES = 8
LANES = 128

ROWS_IN_PROJ = 512
ROWS_MLA_PROJ = 1024
ROWS_OUT_PROJ = 1024
ROWS_XATTN = 1024
ROWS_SCAN = 512
FLASH_Q_ROWS = 1024
FLASH_KV_ROWS = 2048


def _cparams(sem):
    return pltpu.CompilerParams(dimension_semantics=sem, vmem_limit_bytes=VMEM_LIMIT_BYTES)


def _dot(a, b, prec=None):
    return jnp.dot(a, b, preferred_element_type=F32, precision=prec)


def _dot_nt(a, b, prec=None):
    return lax.dot_general(a, b, (((1,), (1,)), ((), ())), preferred_element_type=F32, precision=prec)


def _dot_tn(a, b, prec=None):
    return lax.dot_general(a, b, (((0,), (0,)), ((), ())), preferred_element_type=F32, precision=prec)


def _sigmoid(x):
    return 1.0 / (1.0 + jnp.exp(-x))


def _log_sigmoid(x):
    return jnp.minimum(x, 0.0) - jnp.log(1.0 + jnp.exp(-jnp.abs(x)))


def _rms(x, n):
    return x * lax.rsqrt(jnp.sum(x * x, axis=-1, keepdims=True) * (1.0 / n) + NORM_EPS)


def _head_id(shape, axis, hd=HEAD_DIM):
    return lax.broadcasted_iota(jnp.int32, shape, axis) // hd


def _head_sum_bcast(x, hid, nheads=GROUP_HEADS):
    out = jnp.zeros_like(x)
    for h in range(nheads):
        hm = hid == h
        s = jnp.sum(jnp.where(hm, x, 0.0), axis=-1, keepdims=True)
        out = jnp.where(hm, s, out)
    return out


IN_SPLITS = (MLSTM_PAD, MLA_PAD, RWKV_IN, LRU_IN)
IN_TOTAL = sum(IN_SPLITS)


def _in_proj_kernel(x_ref, g_ref, w_ref, pm_ref, pa_ref, pr_ref, pl_ref):
    x = x_ref[...]
    h = (_rms(x, D_MODEL) * g_ref[...]).astype(BF16)
    off = 0
    for ref, width in zip((pm_ref, pa_ref, pr_ref, pl_ref), IN_SPLITS):
        ref[...] = _dot(h, w_ref[:, off:off + width])
        off += width


def in_proj(x, g, w):
    n = x.shape[0]
    tm = min(ROWS_IN_PROJ, n)
    return pl.pallas_call(
        _in_proj_kernel,
        grid=(n // tm,),
        in_specs=[
            pl.BlockSpec((tm, D_MODEL), lambda i: (i, 0)),
            pl.BlockSpec((1, D_MODEL), lambda i: (0, 0)),
            pl.BlockSpec((D_MODEL, IN_TOTAL), lambda i: (0, 0)),
        ],
        out_specs=[pl.BlockSpec((tm, wd), lambda i: (i, 0)) for wd in IN_SPLITS],
        out_shape=[jax.ShapeDtypeStruct((n, wd), F32) for wd in IN_SPLITS],
        compiler_params=_cparams(("parallel",)),
        name="in_proj",
    )(x, g, w)


def prep_w_in(w_in):
    s1 = MLSTM_IN
    s2 = s1 + MLA_IN
    s3 = s2 + RWKV_IN
    nl = w_in.shape[0]

    def z(width):
        return jnp.zeros((nl, D_MODEL, width), F32)

    w_m = jnp.concatenate([w_in[:, :, :s1], z(MLSTM_PAD - MLSTM_IN)], axis=-1)
    a = w_in[:, :, s1:s2]
    w_a = jnp.concatenate([
        a[:, :, :MLA_Q_LORA], z(256 - MLA_Q_LORA),
        a[:, :, MLA_Q_LORA:MLA_Q_LORA + MLA_KV_LORA],
        z(MLA_NOPE), a[:, :, MLA_Q_LORA + MLA_KV_LORA:], z(128 - MLA_NOPE - MLA_ROPE),
    ], axis=-1)
    return jnp.concatenate([w_m, w_a, w_in[:, :, s2:s3], w_in[:, :, s3:]], axis=-1).astype(BF16)


def _mlstm_kernel(rev, *refs):
    if rev:
        pm_ref, bias_ref, hfw_ref, hg_ref, out_ref, ct_scr, n_scr, m_scr = refs
    else:
        pm_ref, bias_ref, out_ref, ct_scr, n_scr, m_scr = refs
    L = MLSTM_CHUNK

    @pl.when(pl.program_id(1) == 0)
    def _():
        ct_scr[...] = jnp.zeros_like(ct_scr)
        n_scr[...] = jnp.zeros_like(n_scr)
        m_scr[...] = jnp.zeros_like(m_scr)

    p = pm_ref[0]
    q = p[:, 0:GROUP_W]
    k = p[:, GROUP_W:2 * GROUP_W] * (HEAD_DIM ** -0.5)
    v = p[:, 2 * GROUP_W:3 * GROUP_W]
    gt = p[:, 4 * GROUP_W:4 * GROUP_W + LANES] + bias_ref[...]
    ioff, foff = (2 * GROUP_HEADS, 3 * GROUP_HEADS) if rev else (0, GROUP_HEADS)
    logf = _log_sigmoid(gt)
    row = lax.broadcasted_iota(jnp.int32, (L, L), 0)
    col = lax.broadcasted_iota(jnp.int32, (L, L), 1)
    tri = (col >= row) if rev else (col <= row)
    bcum = _dot(tri.astype(F32), logf, HI)
    bcum_t = bcum.T
    gt_t = gt.T
    hid = _head_id((L, GROUP_W), 1)
    hid_row = hid[0:1, :]
    rid = _head_id((GROUP_W, GROUP_W), 0)
    cid = _head_id((GROUP_W, GROUP_W), 1)

    qb = q.astype(BF16)
    kb = k.astype(BF16)
    vb = v.astype(BF16)
    n_row = n_scr[0:1, :]
    qc = _dot(qb, ct_scr[...].astype(BF16))
    qn = _head_sum_bcast(q * n_row, hid)

    acc = jnp.zeros((L, GROUP_W), F32)
    wg_full = jnp.zeros((L, GROUP_W), F32)
    cs_row = jnp.zeros((1, GROUP_W), F32)
    cs_full = jnp.zeros((GROUP_W, GROUP_W), F32)
    for h in range(GROUP_HEADS):
        hm = hid == h
        i_row = gt_t[ioff + h:ioff + h + 1, :]
        i_col = gt[:, ioff + h:ioff + h + 1]
        b_row = bcum_t[foff + h:foff + h + 1, :]
        b_col = bcum[:, foff + h:foff + h + 1]
        m_st = m_scr[h:h + 1, 0:1]
        dmat = jnp.where(tri, b_col - b_row + i_row, -jnp.inf)
        inter = b_col + m_st
        m_row = jnp.maximum(inter, jnp.max(dmat, axis=-1, keepdims=True))
        s = _dot_nt(jnp.where(hm, q, 0.0).astype(BF16), kb) * jnp.exp(dmat - m_row)
        w_inter = jnp.exp(inter - m_row)
        num = _dot(s.astype(BF16), vb) + w_inter * qc
        den = jnp.sum(s, axis=-1, keepdims=True) + w_inter * qn
        hh = num / jnp.maximum(jnp.abs(den), jnp.exp(-m_row))
        acc = jnp.where(hm, hh, acc)

        b_end = b_col[0:1, :] if rev else b_col[L - 1:L, :]
        g = b_end - b_col + i_col
        m_new = jnp.maximum(b_end + m_st, jnp.max(g, axis=0, keepdims=True))
        wg_full = jnp.where(hm, jnp.exp(g - m_new), wg_full)
        cs = jnp.exp(b_end + m_st - m_new)
        cs_row = jnp.where(hid_row == h, cs, cs_row)
        cs_full = jnp.where(rid == h, cs, cs_full)
        m_scr[h:h + 1, :] = jnp.broadcast_to(m_new, (1, LANES))

    kw = k * wg_full
    ct_new = cs_full * ct_scr[...] + _dot_tn(kw.astype(BF16), vb)
    ct_scr[...] = jnp.where(rid == cid, ct_new, 0.0)
    n_scr[0:1, :] = cs_row * n_row + jnp.sum(kw, axis=0, keepdims=True)

    if rev:
        hs = hfw_ref[0] + acc
        ms = _head_sum_bcast(hs * hs, hid) * (1.0 / HEAD_DIM)
        y = hs * lax.rsqrt(ms + NORM_EPS) * hg_ref[...]
        o = p[:, 3 * GROUP_W:4 * GROUP_W]
        out_ref[0] = (y * _sigmoid(o)).astype(out_ref.dtype)
    else:
        out_ref[0] = acc


def mlstm_mixer(pm, gate_b, head_g):
    S, T, _ = pm.shape
    L = MLSTM_CHUNK
    nc = T // L
    bias = jnp.zeros((1, LANES), F32).at[0, :4 * GROUP_HEADS].set(gate_b)
    scratch = [pltpu.VMEM((GROUP_W, GROUP_W), F32), pltpu.VMEM((SUBLANES, GROUP_W), F32),
               pltpu.VMEM((SUBLANES, LANES), F32)]
    h_fw = pl.pallas_call(
        functools.partial(_mlstm_kernel, False),
        grid=(S, nc),
        in_specs=[pl.BlockSpec((1, L, MLSTM_PAD), lambda s, c: (s, c, 0)),
                  pl.BlockSpec((1, LANES), lambda s, c: (0, 0))],
        out_specs=pl.BlockSpec((1, L, GROUP_W), lambda s, c: (s, c, 0)),
        out_shape=jax.ShapeDtypeStruct((S, T, GROUP_W), F32),
        scratch_shapes=scratch,
        compiler_params=_cparams(("parallel", "arbitrary")),
        name="mlstm_fw",
    )(pm, bias)
    return pl.pallas_call(
        functools.partial(_mlstm_kernel, True),
        grid=(S, nc),
        in_specs=[pl.BlockSpec((1, L, MLSTM_PAD), lambda s, c: (s, nc - 1 - c, 0)),
                  pl.BlockSpec((1, LANES), lambda s, c: (0, 0)),
                  pl.BlockSpec((1, L, GROUP_W), lambda s, c: (s, nc - 1 - c, 0)),
                  pl.BlockSpec((1, GROUP_W), lambda s, c: (0, 0))],
        out_specs=pl.BlockSpec((1, L, GROUP_W), lambda s, c: (s, nc - 1 - c, 0)),
        out_shape=jax.ShapeDtypeStruct((S, T, GROUP_W), BF16),
        scratch_shapes=scratch,
        compiler_params=_cparams(("parallel", "arbitrary")),
        name="mlstm_bw",
    )(pm, bias, h_fw, head_g.reshape(1, GROUP_W))


MLA_HEAD_PAD = 128
ROPE_LO = MLA_NOPE
ROPE_HALF = MLA_ROPE // 2
K_NORM_LANE = MLA_QK
Q_NORM_LANE = MLA_QK + 1
NORM_INFLATE = 1.0 + 2.0 ** -6
MAX_LAG_LOG2 = 100.0


def _mla_proj_kernel(pa_ref, qag_ref, wuq_ref, kvag_ref, wuk_ref, wv_ref, qg_ref, kg_ref, c_ref, s_ref,
                     q_ref, k_ref, v_ref):
    pa = pa_ref[0]
    cq = pa[:, 0:256]
    ckv = pa[:, 256:384]
    kpe = pa[:, 384:512]
    cqn = (cq * lax.rsqrt(jnp.sum(cq * cq, axis=-1, keepdims=True) * (1.0 / MLA_Q_LORA) + NORM_EPS)
           * qag_ref[...]).astype(BF16)
    ckvn = (_rms(ckv, MLA_KV_LORA) * kvag_ref[...]).astype(BF16)
    q_all = _dot(cqn, wuq_ref[...])
    k_all = _dot(ckvn, wuk_ref[...])
    v_all = _dot(ckvn, wv_ref[...])
    cos = c_ref[...]
    sin = s_ref[...]
    lane = lax.broadcasted_iota(jnp.int32, cos.shape, 1)
    first_half = lane < ROPE_LO + ROPE_HALF

    def norm_rope(x, g):
        x = x * lax.rsqrt(jnp.sum(x * x, axis=-1, keepdims=True) * (1.0 / MLA_QK) + NORM_EPS) * g
        partner = jnp.where(first_half, pltpu.roll(x, MLA_HEAD_PAD - ROPE_HALF, axis=1),
                            pltpu.roll(x, ROPE_HALF, axis=1))
        return x * cos + partner * sin

    def with_norm(x, norm_lane):
        xb = x.astype(BF16)
        xf = xb.astype(F32)
        norm = jnp.sqrt(jnp.sum(xf * xf, axis=-1, keepdims=True)) * NORM_INFLATE
        return jnp.where(lane == norm_lane, norm.astype(BF16), xb)

    for h in range(GROUP_HEADS):
        sl = slice(h * MLA_HEAD_PAD, (h + 1) * MLA_HEAD_PAD)
        q_ref[0, h] = with_norm(norm_rope(q_all[:, sl], qg_ref[...]) * (MLA_QK ** -0.5 * LOG2E), Q_NORM_LANE)
        k_ref[0, h] = with_norm(norm_rope(k_all[:, sl] + kpe, kg_ref[...]), K_NORM_LANE)
        v_ref[0, h] = jnp.where(lane < HEAD_DIM, v_all[:, sl], 1.0).astype(v_ref.dtype)


def _flash_kernel(q_ref, k_ref, v_ref, o_ref, m_scr, acc_scr, stat_scr):
    j = pl.program_id(3)

    for hh in range(2):
        k_norm_max = jnp.max(k_ref[0, hh], axis=0, keepdims=True)[:, K_NORM_LANE:K_NORM_LANE + 1].astype(F32)

        def q_norm():
            return q_ref[0, hh, :, Q_NORM_LANE:Q_NORM_LANE + 1].astype(F32)

        @pl.when(j == 0)
        def _():
            qn = q_norm()
            m_scr[hh] = -(qn * k_norm_max)
            acc_scr[hh] = jnp.zeros(acc_scr.shape[1:], acc_scr.dtype)
            stat_scr[hh, 0:1, :] = jnp.broadcast_to(jnp.max(qn, axis=0, keepdims=True), (1, LANES))
            stat_scr[hh, 1:2, :] = jnp.broadcast_to(k_norm_max, (1, LANES))

        def one_pass():
            m_prev = m_scr[hh]
            s = _dot_nt(q_ref[0, hh], k_ref[0, hh])
            p = jnp.exp2(s - m_prev).astype(BF16)
            m_new = jnp.maximum(m_prev, jnp.max(s, axis=-1, keepdims=True))
            acc_scr[hh] = (acc_scr[hh] + _dot(p, v_ref[0, hh])) * jnp.exp2(m_prev - m_new)
            m_scr[hh] = m_new

        def two_pass():
            m_prev = m_scr[hh]
            s = _dot_nt(q_ref[0, hh], k_ref[0, hh])
            m_new = jnp.maximum(m_prev, jnp.max(s, axis=-1, keepdims=True))
            p = jnp.exp2(s - m_new).astype(BF16)
            acc_scr[hh] = jnp.exp2(m_prev - m_new) * acc_scr[hh] + _dot(p, v_ref[0, hh])
            m_scr[hh] = m_new

        q_norm_max = stat_scr[hh, 0:1, 0:1]
        k_norm_max0 = stat_scr[hh, 1:2, 0:1]
        surely_ok = jnp.max(q_norm_max * (k_norm_max + k_norm_max0)) <= MAX_LAG_LOG2

        @pl.when(surely_ok)
        def _():
            one_pass()

        @pl.when(jnp.logical_not(surely_ok))
        def _():
            lagged_ok = jnp.max(q_norm() * k_norm_max - m_scr[hh]) <= MAX_LAG_LOG2

            @pl.when(lagged_ok)
            def _():
                one_pass()

            @pl.when(jnp.logical_not(lagged_ok))
            def _():
                two_pass()

    @pl.when(j == pl.num_programs(3) - 1)
    def _():
        a0 = acc_scr[0]
        a1 = acc_scr[1]
        o0 = a0 / a0[:, HEAD_DIM:HEAD_DIM + 1]
        o1 = a1 / a1[:, HEAD_DIM:HEAD_DIM + 1]
        lane = lax.broadcasted_iota(jnp.int32, o0.shape, 1)
        o_ref[0] = jnp.where(lane < HEAD_DIM, o0, pltpu.roll(o1, HEAD_DIM, axis=1)).astype(o_ref.dtype)


def _rope_tables(T):
    inv = 1.0 / (ROPE_THETA ** (jnp.arange(0, MLA_ROPE, 2, dtype=F32) / MLA_ROPE))
    ang = jnp.arange(T, dtype=F32)[:, None] * inv[None, :]
    cos, sin = jnp.cos(ang), jnp.sin(ang)
    pad = jnp.zeros((T, MLA_HEAD_PAD - MLA_QK), F32)
    c = jnp.concatenate([jnp.ones((T, MLA_NOPE), F32), cos, cos, pad], axis=-1)
    s = jnp.concatenate([jnp.zeros((T, MLA_NOPE), F32), -sin, sin, pad], axis=-1)
    return c, s


def prep_mla(qa_g, w_uq, kva_g, w_ukv, q_g, k_g):
    qag = jnp.pad(qa_g, (0, 256 - MLA_Q_LORA)).reshape(1, 256)
    wq = w_uq.reshape(MLA_Q_LORA, GROUP_HEADS, MLA_QK)
    wq = jnp.pad(wq, ((0, 256 - MLA_Q_LORA), (0, 0), (0, MLA_HEAD_PAD - MLA_QK))).reshape(256, -1).astype(BF16)
    wkv = w_ukv.reshape(MLA_KV_LORA, GROUP_HEADS, MLA_NOPE + HEAD_DIM)
    wk = jnp.pad(wkv[:, :, :MLA_NOPE], ((0, 0), (0, 0), (0, MLA_HEAD_PAD - MLA_NOPE)))
    wk = wk.reshape(MLA_KV_LORA, -1).astype(BF16)
    wv = jnp.pad(wkv[:, :, MLA_NOPE:], ((0, 0), (0, 0), (0, MLA_HEAD_PAD - HEAD_DIM)))
    wv = wv.reshape(MLA_KV_LORA, -1).astype(BF16)
    qg = jnp.pad(q_g, (0, MLA_HEAD_PAD - MLA_QK)).reshape(1, -1)
    kg = jnp.pad(k_g, (0, MLA_HEAD_PAD - MLA_QK)).reshape(1, -1)
    return qag, wq, kva_g.reshape(1, -1), wk, wv, qg, kg


def mla_mixer(pa, qa_g, w_uq, kva_g, w_ukv, q_g, k_g, tq=FLASH_Q_ROWS, tk=FLASH_KV_ROWS):
    S, T, _ = pa.shape
    tm = min(ROWS_MLA_PROJ, T)
    tq = min(tq, T)
    tk = min(tk, T)
    qag, wq, kvag, wk, wv, qg, kg = prep_mla(qa_g, w_uq, kva_g, w_ukv, q_g, k_g)
    cos, sin = _rope_tables(T)
    full = lambda a: pl.BlockSpec(a.shape, lambda s, i: (0,) * a.ndim)
    head_blk = pl.BlockSpec((1, GROUP_HEADS, tm, MLA_HEAD_PAD), lambda s, i: (s, 0, i, 0))
    head_shape = jax.ShapeDtypeStruct((S, GROUP_HEADS, T, MLA_HEAD_PAD), BF16)
    q, k, v = pl.pallas_call(
        _mla_proj_kernel,
        grid=(S, T // tm),
        in_specs=[pl.BlockSpec((1, tm, MLA_PAD), lambda s, i: (s, i, 0)),
                  full(qag), full(wq), full(kvag), full(wk), full(wv), full(qg), full(kg),
                  pl.BlockSpec((tm, MLA_HEAD_PAD), lambda s, i: (i, 0)),
                  pl.BlockSpec((tm, MLA_HEAD_PAD), lambda s, i: (i, 0))],
        out_specs=[head_blk, head_blk, head_blk],
        out_shape=[head_shape, head_shape, head_shape],
        compiler_params=_cparams(("parallel", "parallel")),
        name="mla_proj",
    )(pa, qag, wq, kvag, wk, wv, qg, kg, cos, sin)
    kv_blk = pl.BlockSpec((1, 2, tk, MLA_HEAD_PAD), lambda s, p, i, j: (s, p, j, 0))
    return pl.pallas_call(
        _flash_kernel,
        grid=(S, 2, T // tq, T // tk),
        in_specs=[pl.BlockSpec((1, 2, tq, MLA_HEAD_PAD), lambda s, p, i, j: (s, p, i, 0)), kv_blk, kv_blk],
        out_specs=pl.BlockSpec((1, tq, 2 * HEAD_DIM), lambda s, p, i, j: (s, i, p)),
        out_shape=jax.ShapeDtypeStruct((S, T, GROUP_W), BF16),
        scratch_shapes=[pltpu.VMEM((2, tq, 1), F32), pltpu.VMEM((2, tq, MLA_HEAD_PAD), F32),
                        pltpu.VMEM((2, SUBLANES, LANES), F32)],
        compiler_params=_cparams(("parallel", "parallel", "parallel", "arbitrary")),
        name="mla_flash",
    )(q, k, v)


HALO = SUBLANES
RWKV_SMALL_OFF = 3 * GROUP_W


def _mm(a, b):
    return _dot(a.astype(BF16), b.astype(BF16))


def _mm_nt(a, b):
    return _dot_nt(a.astype(BF16), b.astype(BF16))


def _mm_tn(a, b):
    return _dot_tn(a.astype(BF16), b.astype(BF16))


def _shifted_rows(x, prev_row, next_row):
    n = x.shape[0]
    rowi = lax.broadcasted_iota(jnp.int32, x.shape, 0)
    prev = jnp.where(rowi == 0, prev_row, pltpu.roll(x, 1, axis=0))
    nxt = jnp.where(rowi == n - 1, next_row, pltpu.roll(x, n - 1, axis=0))
    return prev, nxt


def _rwkv_prep(x, prev_row, next_row, mu_ref, kk_ref):
    prev, nxt = _shifted_rows(x, prev_row, next_row)
    pf = x + mu_ref[0:1, :] * (prev - x) + mu_ref[1:2, :] * (nxt - x)
    r = pf[:, 0:GROUP_W]
    k = pf[:, GROUP_W:2 * GROUP_W]
    v = pf[:, 2 * GROUP_W:3 * GROUP_W]
    z = pf[:, RWKV_SMALL_OFF:RWKV_SMALL_OFF + LANES]
    kk = k * kk_ref[...]
    kk = kk * lax.rsqrt(_head_sum_bcast(kk * kk, _head_id(kk.shape, 1)) + 1e-12)
    return r, k, v, z, kk


def _rwkv_direction(d, z, zt, k, w0_ref, ww2_ref, a0_ref, wa2_ref, ka_ref):
    wz = w0_ref[d:d + 1, :] + _mm(zt, ww2_ref[d])
    logw = -math.exp(-0.5) * _sigmoid(wz)
    a = _sigmoid(a0_ref[d:d + 1, :] + _mm(z, wa2_ref[d]))
    kd = k * (1.0 + (a - 1.0) * ka_ref[...])
    return logw, a, kd


def _rwkv_scan_kernel(tb, xf_ref, hpf_ref, hnf_ref, xb_ref, hpb_ref, hnb_ref, mu_ref, w0_ref, ww2_ref, a0_ref,
                      wa2_ref, kk_ref, ka_ref, yf_ref, yb_ref, z_scr, r_scr, lw_scr, k_scr, v_scr, a_scr, b_scr):
    L = RWKV_CHUNK
    nch = tb // L
    nb = pl.num_programs(1)
    i = pl.program_id(1)

    @pl.when(i == 0)
    def _():
        z_scr[...] = jnp.zeros_like(z_scr)

    for d, (x_ref, hp_ref, hn_ref) in enumerate(((xf_ref, hpf_ref, hnf_ref), (xb_ref, hpb_ref, hnb_ref))):
        ib = i if d == 0 else nb - 1 - i
        prev_row = jnp.where(ib == 0, 0.0, hp_ref[0, HALO - 1:HALO, :])
        next_row = jnp.where(ib == nb - 1, 0.0, hn_ref[0, 0:1, :])
        r, k, v, z, kk = _rwkv_prep(x_ref[0], prev_row, next_row, mu_ref, kk_ref)
        logw, a_d, k_d = _rwkv_direction(d, z, jnp.tanh(z), k, w0_ref, ww2_ref, a0_ref, wa2_ref, ka_ref)
        r_scr[d] = r
        lw_scr[d] = logw
        k_scr[d] = k_d
        v_scr[d] = v
        a_scr[d] = -kk
        b_scr[d] = kk * a_d

    rowc = lax.broadcasted_iota(jnp.int32, (L, GROUP_W), 0)
    n4 = GROUP_HEADS * L
    tr = lax.broadcasted_iota(jnp.int32, (n4, n4), 0)
    tc = lax.broadcasted_iota(jnp.int32, (n4, n4), 1)
    trm = tr % L
    tcm = tc % L
    eye = (tr == tc).astype(F32)
    hid_l = _head_id((L, GROUP_W), 1)

    def stack(t):
        return jnp.concatenate([jnp.where(hid_l == h, t, 0.0) for h in range(GROUP_HEADS)], axis=0)

    def operands(d, cc):
        rev = d == 1
        strict = (tcm > trm) if rev else (tcm < trm)
        incl = (tcm >= trm) if rev else (tcm <= trm)
        rows = pl.ds(pl.multiple_of(cc * L, L), L)
        lw = lw_scr[d, rows, :]
        cs = lw
        s = 1
        while s < L:
            shifted = pltpu.roll(cs, (L - s) if rev else s, axis=0)
            cs = cs + jnp.where((rowc < L - s) if rev else (rowc >= s), shifted, 0.0)
            s *= 2
        tot = cs[0:1, :] if rev else cs[L - 1:L, :]
        ep = jnp.exp(cs)
        en = jnp.exp(-cs)
        ee = jnp.exp(tot - cs)
        kc = k_scr[d, rows, :]
        bc = b_scr[d, rows, :]
        a_s = stack(a_scr[d, rows, :] * jnp.exp(cs - lw))
        b_s = stack(bc * en)
        k_s = stack(kc * en)
        r_s = stack(r_scr[d, rows, :] * ep)
        o = dict(rows=rows, a_s=a_s, r_s=r_s, v_s=stack(v_scr[d, rows, :]), be_s=stack(bc * ee),
                 ke_s=stack(kc * ee), p_tot=jnp.exp(tot))
        o["aab"] = jnp.where(strict, _mm_nt(a_s, b_s), 0.0)
        o["aak"] = jnp.where(strict, _mm_nt(a_s, k_s), 0.0)
        o["arb"] = jnp.where(incl, _mm_nt(r_s, b_s), 0.0)
        o["ark"] = jnp.where(incl, _mm_nt(r_s, k_s), 0.0)
        return o

    dirs = (0, 1)
    y_refs = (yf_ref, yb_ref)

    def both(c, carry):
        o = [operands(0, c), operands(1, nch - 1 - c)]
        inv = [eye + o[d]["aab"] for d in dirs]
        pw = [o[d]["aab"] for d in dirs]
        for _ in range(5):
            pw = [_mm(pw[d], pw[d]) for d in dirs]
            inv = [inv[d] + _mm(inv[d], pw[d]) for d in dirs]
        s0 = [z_scr[d] for d in dirs]
        x = [_mm_nt(o[d]["a_s"], s0[d]) + _mm(o[d]["aak"], o[d]["v_s"]) for d in dirs]
        u = [_mm(inv[d], x[d]) for d in dirs]
        y_s = [_mm_nt(o[d]["r_s"], s0[d]) + _mm(o[d]["arb"], u[d]) + _mm(o[d]["ark"], o[d]["v_s"]) for d in dirs]
        for d in dirs:
            y_refs[d][0, o[d]["rows"], :] = (y_s[d][0:L] + y_s[d][L:2 * L] + y_s[d][2 * L:3 * L]
                                             + y_s[d][3 * L:4 * L])
            z_scr[d] = o[d]["p_tot"] * s0[d] + _mm_tn(u[d], o[d]["be_s"]) + _mm_tn(o[d]["v_s"], o[d]["ke_s"])
        return carry

    lax.fori_loop(0, nch, both, 0)


def _rwkv_final_kernel(x_ref, hp_ref, hn_ref, mu_ref, w0_ref, ww2_ref, a0_ref, wa2_ref, wg2_ref, kk_ref, ka_ref,
                       rk_ref, lng_ref, lnb_ref, yf_ref, yb_ref, out_ref):
    nb = pl.num_programs(1)
    ib = pl.program_id(1)
    prev_row = jnp.where(ib == 0, 0.0, hp_ref[0, HALO - 1:HALO, :])
    next_row = jnp.where(ib == nb - 1, 0.0, hn_ref[0, 0:1, :])
    r, k, v, z, _ = _rwkv_prep(x_ref[0], prev_row, next_row, mu_ref, kk_ref)
    zt = jnp.tanh(z)
    _, _, k_0 = _rwkv_direction(0, z, zt, k, w0_ref, ww2_ref, a0_ref, wa2_ref, ka_ref)
    _, _, k_1 = _rwkv_direction(1, z, zt, k, w0_ref, ww2_ref, a0_ref, wa2_ref, ka_ref)
    hid = _head_id(r.shape, 1)
    y = yf_ref[0] + yb_ref[0]
    mean = _head_sum_bcast(y, hid) * (1.0 / HEAD_DIM)
    yc = y - mean
    var = _head_sum_bcast(yc * yc, hid) * (1.0 / HEAD_DIM)
    y = yc * lax.rsqrt(var + RWKV_GN_EPS) * lng_ref[...] + lnb_ref[...]
    bonus = _head_sum_bcast(r * (k_0 + k_1) * rk_ref[...], hid) * v
    gate = _mm(_sigmoid(z), wg2_ref[...])
    out_ref[0] = ((y + bonus) * gate).astype(out_ref.dtype)


def rwkv_mixer(pr, mu, w0, w_w2, a0, w_a2, w_g2, k_k, k_a, r_k, ln_g, ln_b, tb=ROWS_SCAN):
    S, T, _ = pr.shape
    tb = min(tb, T)
    nb = T // tb
    hb = tb // HALO
    nh = T // HALO

    def lora_pad(w, lo):
        r = w.shape[-2]
        pad = [(0, 0)] * (w.ndim - 2) + [(lo, LANES - lo - r), (0, 0)]
        return jnp.pad(w, pad).astype(BF16)

    ww2 = lora_pad(w_w2, 0)
    wa2 = lora_pad(w_a2, 32)
    wg2 = lora_pad(w_g2, 64)
    row = lambda t: t.reshape(1, GROUP_W)
    full = lambda t: pl.BlockSpec(t.shape, lambda s, i: (0,) * t.ndim)

    def specs(blk):
        return [pl.BlockSpec((1, tb, RWKV_IN), lambda s, i: (s, blk(i), 0)),
                pl.BlockSpec((1, HALO, RWKV_IN), lambda s, i: (s, jnp.maximum(blk(i) * hb - 1, 0), 0)),
                pl.BlockSpec((1, HALO, RWKV_IN), lambda s, i: (s, jnp.minimum((blk(i) + 1) * hb, nh - 1), 0))]

    fwd = lambda i: i
    bwd = lambda i: nb - 1 - i
    params = (mu, w0, ww2, a0, wa2, row(k_k), row(k_a))
    y_shape = jax.ShapeDtypeStruct((S, T, GROUP_W), F32)
    y_fw, y_bw = pl.pallas_call(
        functools.partial(_rwkv_scan_kernel, tb),
        grid=(S, nb),
        in_specs=specs(fwd) + specs(bwd) + [full(t) for t in params],
        out_specs=[pl.BlockSpec((1, tb, GROUP_W), lambda s, i: (s, i, 0)),
                   pl.BlockSpec((1, tb, GROUP_W), lambda s, i: (s, nb - 1 - i, 0))],
        out_shape=[y_shape, y_shape],
        scratch_shapes=[pltpu.VMEM((2, GROUP_W, GROUP_W), F32)]
        + [pltpu.VMEM((2, tb, GROUP_W), F32) for _ in range(6)],
        compiler_params=_cparams(("parallel", "arbitrary")),
        name="rwkv_scan",
    )(pr, pr, pr, pr, pr, pr, *params)
    fparams = (mu, w0, ww2, a0, wa2, wg2, row(k_k), row(k_a), row(r_k), row(ln_g), row(ln_b))
    yblk = pl.BlockSpec((1, tb, GROUP_W), lambda s, i: (s, i, 0))
    return pl.pallas_call(
        _rwkv_final_kernel,
        grid=(S, nb),
        in_specs=specs(fwd) + [full(t) for t in fparams] + [yblk, yblk],
        out_specs=yblk,
        out_shape=jax.ShapeDtypeStruct((S, T, GROUP_W), BF16),
        compiler_params=_cparams(("parallel", "parallel")),
        name="rwkv_final",
    )(pr, pr, pr, *fparams, y_fw, y_bw)


def _gelu_tanh(x):
    return 0.5 * x * (1.0 + jnp.tanh(math.sqrt(2.0 / math.pi) * (x + 0.044715 * (x * x * x))))


def _lru_kernel(rev, tb, *refs):
    if rev:
        x_ref, hp_ref, hn_ref, cw_ref, cb_ref, gw_ref, gb_ref, lam_ref, hfw_ref, out_ref, h_scr = refs
    else:
        x_ref, hp_ref, hn_ref, cw_ref, cb_ref, gw_ref, gb_ref, lam_ref, out_ref, h_scr = refs
    nb = pl.num_programs(1)
    ib = (nb - 1 - pl.program_id(1)) if rev else pl.program_id(1)

    @pl.when(pl.program_id(1) == 0)
    def _():
        h_scr[...] = jnp.zeros_like(h_scr)

    xb = x_ref[0, :, 0:GROUP_W]
    first = ib == 0
    last = ib == nb - 1
    pm2 = jnp.where(first, 0.0, hp_ref[0, HALO - 2:HALO - 1, 0:GROUP_W])
    pm1 = jnp.where(first, 0.0, hp_ref[0, HALO - 1:HALO, 0:GROUP_W])
    nx1 = jnp.where(last, 0.0, hn_ref[0, 0:1, 0:GROUP_W])
    rowi = lax.broadcasted_iota(jnp.int32, (tb, GROUP_W), 0)
    x_m1 = jnp.where(rowi == 0, pm1, pltpu.roll(xb, 1, axis=0))
    x_m2 = jnp.where(rowi == 0, pm2, jnp.where(rowi == 1, pm1, pltpu.roll(xb, 2, axis=0)))
    x_p1 = jnp.where(rowi == tb - 1, nx1, pltpu.roll(xb, tb - 1, axis=0))
    xc = (cb_ref[...] + x_m2 * cw_ref[0:1, :] + x_m1 * cw_ref[1:2, :] + xb * cw_ref[2:3, :]
          + x_p1 * cw_ref[3:4, :])
    xcb = xc.astype(BF16)
    r_gate = _sigmoid(_dot(xcb, gw_ref[0]) + gb_ref[0:1, :])
    i_gate = _sigmoid(_dot(xcb, gw_ref[1]) + gb_ref[1:2, :])
    lam = lam_ref[...]
    softplus_neg_lam = jnp.maximum(-lam, 0.0) + jnp.log(1.0 + jnp.exp(-jnp.abs(lam)))
    log_a = -LRU_C * r_gate * softplus_neg_lam
    a = jnp.exp(log_a)
    u = jnp.sqrt(1.0 - jnp.exp(2.0 * log_a)) * (i_gate * xc)

    s = 1
    while s < tb:
        shift = (tb - s) if rev else s
        valid = (rowi < tb - s) if rev else (rowi >= s)
        a_sh = pltpu.roll(a, shift, axis=0)
        u_sh = pltpu.roll(u, shift, axis=0)
        u = jnp.where(valid, a * u_sh + u, u)
        a = jnp.where(valid, a * a_sh, a)
        s *= 2
    h = a * h_scr[0:1, :] + u
    h_scr[0:1, :] = h[0:1, :] if rev else h[tb - 1:tb, :]

    if rev:
        out_ref[0] = ((hfw_ref[0] + h) * _gelu_tanh(x_ref[0, :, GROUP_W:2 * GROUP_W])).astype(out_ref.dtype)
    else:
        out_ref[0] = h


def lru_mixer(pl_in, conv_w, conv_b, gate_w, gate_b, lam, tb=ROWS_SCAN):
    S, T, _ = pl_in.shape
    tb = min(tb, T)
    nb = T // tb
    hb = tb // HALO
    nh = T // HALO
    eye = jnp.eye(GROUP_HEADS, dtype=F32)
    gw = jnp.einsum("dgnij,nm->dgnimj", gate_w, eye).reshape(2, 2, GROUP_W, GROUP_W).astype(BF16)
    full = lambda t: pl.BlockSpec(t.shape, lambda s, i: (0,) * t.ndim)

    def specs(rev):
        blk = (lambda i: nb - 1 - i) if rev else (lambda i: i)
        return [pl.BlockSpec((1, tb, LRU_IN), lambda s, i: (s, blk(i), 0)),
                pl.BlockSpec((1, HALO, LRU_IN), lambda s, i: (s, jnp.maximum(blk(i) * hb - 1, 0), 0)),
                pl.BlockSpec((1, HALO, LRU_IN), lambda s, i: (s, jnp.minimum((blk(i) + 1) * hb, nh - 1), 0))]

    def args(d):
        return (conv_w, conv_b.reshape(1, GROUP_W), gw[d], gate_b[d], lam[d].reshape(1, GROUP_W))

    scratch = [pltpu.VMEM((SUBLANES, GROUP_W), F32)]
    h_fw = pl.pallas_call(
        functools.partial(_lru_kernel, False, tb),
        grid=(S, nb),
        in_specs=specs(False) + [full(t) for t in args(0)],
        out_specs=pl.BlockSpec((1, tb, GROUP_W), lambda s, i: (s, i, 0)),
        out_shape=jax.ShapeDtypeStruct((S, T, GROUP_W), F32),
        scratch_shapes=scratch,
        compiler_params=_cparams(("parallel", "arbitrary")),
        name="lru_fw",
    )(pl_in, pl_in, pl_in, *args(0))
    return pl.pallas_call(
        functools.partial(_lru_kernel, True, tb),
        grid=(S, nb),
        in_specs=specs(True) + [full(t) for t in args(1)]
        + [pl.BlockSpec((1, tb, GROUP_W), lambda s, i: (s, nb - 1 - i, 0))],
        out_specs=pl.BlockSpec((1, tb, GROUP_W), lambda s, i: (s, nb - 1 - i, 0)),
        out_shape=jax.ShapeDtypeStruct((S, T, GROUP_W), BF16),
        scratch_shapes=scratch,
        compiler_params=_cparams(("parallel", "arbitrary")),
        name="lru_bw",
    )(pl_in, pl_in, pl_in, *args(1), h_fw)


def _out_proj_kernel(x_ref, ym_ref, ya_ref, yr_ref, yl_ref, w_ref, o_ref):
    acc = x_ref[...]
    for i, y_ref in enumerate((ym_ref, ya_ref, yr_ref, yl_ref)):
        acc = acc + _dot(y_ref[...], w_ref[i * GROUP_W:(i + 1) * GROUP_W, :])
    o_ref[...] = acc


def out_proj(x, ys, w):
    n = x.shape[0]
    tm = min(ROWS_OUT_PROJ, n)
    return pl.pallas_call(
        _out_proj_kernel,
        grid=(n // tm,),
        in_specs=[pl.BlockSpec((tm, D_MODEL), lambda i: (i, 0))]
        + [pl.BlockSpec((tm, GROUP_W), lambda i: (i, 0)) for _ in ys]
        + [pl.BlockSpec((D_MODEL, D_MODEL), lambda i: (0, 0))],
        out_specs=pl.BlockSpec((tm, D_MODEL), lambda i: (i, 0)),
        out_shape=jax.ShapeDtypeStruct((n, D_MODEL), F32),
        compiler_params=_cparams(("parallel",)),
        name="out_proj",
    )(x, *ys, w)


XA_W = XA_HEADS * XA_DIM


def _mem_kv_kernel(m_ref, g_ref, wkv_ref, kg_ref, k_ref, v_ref):
    hm = (_rms(m_ref[0], D_MODEL) * g_ref[...]).astype(BF16)
    kv = _dot(hm, wkv_ref[...])
    k = kv[:, 0:XA_W]
    hid = _head_id(k.shape, 1, XA_DIM)
    k = k * lax.rsqrt(_head_sum_bcast(k * k, hid, XA_HEADS) * (1.0 / XA_DIM) + NORM_EPS) * kg_ref[...]
    k_ref[0] = k.astype(k_ref.dtype)
    v_ref[0] = kv[:, XA_W:2 * XA_W].astype(v_ref.dtype)


def mem_kv(mem, g, wkv, k_g):
    S, M, D = mem.shape
    full = lambda t: pl.BlockSpec(t.shape, lambda s: (0,) * t.ndim)
    args = (g.reshape(1, D), wkv.astype(BF16), jnp.tile(k_g, XA_HEADS).reshape(1, XA_W))
    return pl.pallas_call(
        _mem_kv_kernel,
        grid=(S,),
        in_specs=[pl.BlockSpec((1, M, D), lambda s: (s, 0, 0))] + [full(t) for t in args],
        out_specs=[pl.BlockSpec((1, M, XA_W), lambda s: (s, 0, 0))] * 2,
        out_shape=[jax.ShapeDtypeStruct((S, M, XA_W), BF16)] * 2,
        compiler_params=_cparams(("parallel",)),
        name="mem_kv",
    )(mem, *args)


def _xattn_kernel(x_ref, g_ref, wq_ref, qg_ref, k_ref, v_ref, wo_ref, gf_ref, rt_ref, x2_ref, aff_ref):
    x = x_ref[0]
    h = (_rms(x, D_MODEL) * g_ref[...]).astype(BF16)
    q = _dot(h, wq_ref[...])
    hid = _head_id(q.shape, 1, XA_DIM)
    q = q * lax.rsqrt(_head_sum_bcast(q * q, hid, XA_HEADS) * (1.0 / XA_DIM) + NORM_EPS) * qg_ref[...]
    q = q * (XA_DIM ** -0.5)
    k = k_ref[0]
    v = v_ref[0]
    o = jnp.zeros(q.shape, F32)
    for hh in range(XA_HEADS):
        hm = hid == hh
        s = _dot_nt(jnp.where(hm, q, 0.0).astype(BF16), k)
        p = jnp.exp(s - jnp.max(s, axis=-1, keepdims=True))
        oh = _dot(p.astype(BF16), v) / jnp.sum(p, axis=-1, keepdims=True)
        o = jnp.where(hm, oh, o)
    x2 = x + _dot(o.astype(BF16), wo_ref[...])
    x2_ref[0] = x2
    h3 = _rms(x2, D_MODEL) * gf_ref[...]
    h_hi = h3.astype(BF16)
    h_lo = (h3 - h_hi.astype(F32)).astype(BF16)
    hi_part = _dot_nt(rt_ref[...], h_hi)
    logits = hi_part[0:N_EXPERTS] + hi_part[N_EXPERTS:2 * N_EXPERTS] + _dot_nt(rt_ref[0:N_EXPERTS, :], h_lo)
    e = jnp.exp(logits - jnp.max(logits, axis=0, keepdims=True))
    aff_ref[...] = e / jnp.sum(e, axis=0, keepdims=True)


def xattn_router(x, g_xa, wq, q_g, k, v, wo, g_ffn, router):
    S, T, D = x.shape
    tm = min(ROWS_XATTN, T)
    nb = T // tm
    args = (g_xa.reshape(1, D), wq.astype(BF16), jnp.tile(q_g, XA_HEADS).reshape(1, XA_W))
    rt = router.T
    rt_hi = rt.astype(BF16)
    rt_lo = (rt - rt_hi.astype(F32)).astype(BF16)
    args2 = (wo.astype(BF16), g_ffn.reshape(1, D), jnp.concatenate([rt_hi, rt_lo], axis=0))
    full = lambda t: pl.BlockSpec(t.shape, lambda s, i: (0,) * t.ndim)
    M = k.shape[1]
    return pl.pallas_call(
        _xattn_kernel,
        grid=(S, nb),
        in_specs=[pl.BlockSpec((1, tm, D), lambda s, i: (s, i, 0))] + [full(t) for t in args]
        + [pl.BlockSpec((1, M, XA_W), lambda s, i: (s, 0, 0))] * 2 + [full(t) for t in args2],
        out_specs=[pl.BlockSpec((1, tm, D), lambda s, i: (s, i, 0)),
                   pl.BlockSpec((N_EXPERTS, tm), lambda s, i: (0, s * nb + i))],
        out_shape=[jax.ShapeDtypeStruct((S, T, D), F32), jax.ShapeDtypeStruct((N_EXPERTS, S * T), F32)],
        compiler_params=_cparams(("parallel", "parallel")),
        name="xattn_router",
    )(x, *args, k, v, *args2)


def _excl_prefix(m):
    nb = m.shape[0]
    li = lax.broadcasted_iota(jnp.int32, (LANES, LANES), 0)
    lj = lax.broadcasted_iota(jnp.int32, (LANES, LANES), 1)
    within = _dot(m.astype(BF16), (li < lj).astype(BF16))
    tot = jnp.broadcast_to(jnp.sum(m, axis=-1, keepdims=True), (nb, LANES))
    bi = lax.broadcasted_iota(jnp.int32, (nb, nb), 0)
    bj = lax.broadcasted_iota(jnp.int32, (nb, nb), 1)
    carry = _dot((bj < bi).astype(F32), tot, HI)
    carry_row = _dot_tn(tot[:, 0:SUBLANES], (bi < bj).astype(F32), HI)
    return within + carry, carry_row


def _route_select_kernel(cap, aff_ref, sel_ref, pos_ref, carry_ref):
    a = aff_ref[0]
    bits = pltpu.bitcast(a, jnp.int32)

    def bisect(i, prefix):
        cand = prefix | jnp.left_shift(jnp.int32(1), 30 - i)
        ge = (bits >= cand).astype(F32)
        cnt = jnp.sum(jnp.sum(ge, axis=-1, keepdims=True), axis=0, keepdims=True)
        return jnp.where(cnt >= cap, cand, prefix)

    thr = lax.fori_loop(0, 31, bisect, jnp.zeros((1, 1), jnp.int32))
    gt = bits > thr
    eq = (bits == thr).astype(F32)
    n_gt = jnp.sum(jnp.sum(gt.astype(F32), axis=-1, keepdims=True), axis=0, keepdims=True)
    eq_rank, _ = _excl_prefix(eq)
    sel = jnp.where(gt, 1.0, jnp.where(eq_rank < cap - n_gt, eq, 0.0))
    pos, carry_row = _excl_prefix(sel)
    sel_ref[0] = sel
    pos_ref[0] = pos
    carry_ref[0] = carry_row.astype(jnp.int32)


def route_select(aff, cap):
    E, ng = aff.shape
    nb = ng // LANES
    blk = pl.BlockSpec((1, nb, LANES), lambda e: (e, 0, 0))
    sel, pos, carry = pl.pallas_call(
        functools.partial(_route_select_kernel, cap),
        grid=(E,),
        in_specs=[blk],
        out_specs=[blk, blk, pl.BlockSpec((1, SUBLANES, nb), lambda e: (e, 0, 0))],
        out_shape=[jax.ShapeDtypeStruct((E, nb, LANES), F32), jax.ShapeDtypeStruct((E, nb, LANES), F32),
                   jax.ShapeDtypeStruct((E, SUBLANES, nb), jnp.int32)],
        compiler_params=_cparams(("parallel",)),
        name="route_select",
    )(aff.reshape(E, nb, LANES))
    return sel, pos, carry[:, 0, :]


MOE_TILE = 256
MOE_STRIP = 64
MOE_BLOCK = 512
ROW_ALIGN = 16


def _round_up(x, m):
    return (x + m - 1) // m * m


def _lane_row(ref, e):
    return ref[e, 0]


def _moe_dispatch_kernel(flush_after, off_ref, nsub_ref, used_ref, x_ref, g_ref, sel_ref, pos_ref, aff_ref,
                         xe_hbm, ge_hbm, strip, gstrip, hbuf, obuf, ogbuf, zbuf, zgbuf, sem_x, sem_g, sem_o):
    t = pl.program_id(0)
    nt = pl.num_programs(0)
    slot = t % 2
    R = MOE_STRIP
    sub = lax.broadcasted_iota(jnp.int32, (R, MOE_TILE), 0).astype(F32)

    def strip_copies(tt, sl):
        cps = []
        for e in range(N_EXPERTS):
            row0 = pl.multiple_of(off_ref[e, tt], ROW_ALIGN)
            cps.append(pltpu.make_async_copy(strip.at[sl, pl.ds(e * R, R), :], xe_hbm.at[pl.ds(row0, R), :],
                                             sem_x.at[sl]))
            cps.append(pltpu.make_async_copy(gstrip.at[sl, pl.ds(e * R, R), :], ge_hbm.at[pl.ds(row0, R), :],
                                             sem_g.at[sl]))
        return cps

    def onehot_and_gate(e, s):
        sel = _lane_row(sel_ref, e)
        pos = _lane_row(pos_ref, e)
        lpos = pos - pos[:, 0:1] - jnp.asarray(s * R, F32)
        oh = jnp.where(lpos == sub, sel, 0.0)
        gate = jnp.sum(oh * _lane_row(aff_ref, e), axis=-1, keepdims=True)
        return oh.astype(BF16), jnp.broadcast_to(gate, (R, LANES))

    hbuf[...] = (_rms(x_ref[...], D_MODEL) * g_ref[...]).astype(BF16)
    parts = [onehot_and_gate(e, 0) for e in range(N_EXPERTS)]
    strip[slot] = _dot(jnp.concatenate([p[0] for p in parts], axis=0), hbuf[...]).astype(BF16)
    gstrip[slot] = jnp.concatenate([p[1] for p in parts], axis=0)

    @pl.when(t > 0)
    def _():
        for cp in strip_copies(t - 1, 1 - slot):
            cp.wait()

    def flush(group):
        _, g, row_base, cpad = group
        zbuf[...] = jnp.zeros_like(zbuf)
        zgbuf[...] = jnp.zeros_like(zgbuf)

        def zero_copies(row0):
            row0 = pl.multiple_of(row0, ROW_ALIGN)
            return (pltpu.make_async_copy(zbuf, xe_hbm.at[pl.ds(row0, MOE_BLOCK), :], sem_o.at[0]),
                    pltpu.make_async_copy(zgbuf, ge_hbm.at[pl.ds(row0, MOE_BLOCK), :], sem_o.at[1]))

        heads = [cp for e in range(N_EXPERTS) for cp in zero_copies(used_ref[e, g])]
        for cp in heads:
            cp.start()
        for cp in heads:
            cp.wait()
        for wait in (False, True):
            for e in range(N_EXPERTS):
                first_block = (used_ref[e, g] + MOE_BLOCK - 1) // MOE_BLOCK * MOE_BLOCK
                nblocks = (row_base + (e + 1) * cpad - first_block) // MOE_BLOCK

                def block(i, carry):
                    for cp in zero_copies(first_block + i * MOE_BLOCK):
                        cp.wait() if wait else cp.start()
                    return carry
                lax.fori_loop(0, nblocks, block, 0)

    for group in flush_after[:-1]:
        @pl.when(t == group[0] + 1)
        def _():
            flush(group)

    for cp in strip_copies(t, slot):
        cp.start()

    for e in range(N_EXPERTS):
        def extra(s, carry):
            oh, gate = onehot_and_gate(e, s)
            obuf[...] = _dot(oh, hbuf[...]).astype(BF16)
            ogbuf[...] = gate
            row0 = pl.multiple_of(off_ref[e, t] + s * R, ROW_ALIGN)
            cx = pltpu.make_async_copy(obuf, xe_hbm.at[pl.ds(row0, R), :], sem_o.at[0])
            cg = pltpu.make_async_copy(ogbuf, ge_hbm.at[pl.ds(row0, R), :], sem_o.at[1])
            cx.start()
            cg.start()
            cx.wait()
            cg.wait()
            return carry
        lax.fori_loop(1, nsub_ref[e, t], extra, 0)

    @pl.when(t == nt - 1)
    def _():
        for cp in strip_copies(t, slot):
            cp.wait()
        flush(flush_after[-1])


def _moe_ffn_kernel(blk_ref, valid_ref, xe_ref, ge_ref, wg_ref, wu_ref, wd_ref, y_ref):
    del blk_ref
    valid = valid_ref[pl.program_id(0), pl.program_id(1)] == 1

    @pl.when(valid)
    def _():
        h = xe_ref[...]
        gate_proj = _dot(h, wg_ref[0])
        hid = gate_proj * _sigmoid(gate_proj) * _dot(h, wu_ref[0])
        y_ref[...] = (_dot(hid.astype(BF16), wd_ref[0]) * ge_ref[:, 0:1]).astype(y_ref.dtype)

    @pl.when(jnp.logical_not(valid))
    def _():
        y_ref[...] = jnp.zeros_like(y_ref)


def _moe_combine_kernel(off_ref, nsub_ref, x_ref, selt_ref, post_ref, y_hbm, o_ref, ybuf, obuf, sem_y, sem_o):
    t = pl.program_id(0)
    nt = pl.num_programs(0)
    slot = t % 2
    R = MOE_STRIP

    def fetch(tt, sl):
        return [pltpu.make_async_copy(y_hbm.at[pl.ds(pl.multiple_of(off_ref[e, tt], ROW_ALIGN), R), :],
                                      ybuf.at[sl, pl.ds(e * R, R), :], sem_y.at[sl])
                for e in range(N_EXPERTS)]

    @pl.when(t == 0)
    def _():
        for cp in fetch(0, 0):
            cp.start()

    @pl.when(t + 1 < nt)
    def _():
        for cp in fetch(t + 1, 1 - slot):
            cp.start()

    lane = lax.broadcasted_iota(jnp.int32, (MOE_TILE, LANES), 1).astype(F32)

    def local_pos(e):
        return post_ref[:, e:e + 1] - post_ref[0:1, e:e + 1]

    tiles = []
    for i in range(N_EXPERTS // 2):
        e0, e1 = 2 * i, 2 * i + 1
        w0 = jnp.where(local_pos(e0) == lane, selt_ref[:, e0:e0 + 1], 0.0)
        w1 = jnp.where(local_pos(e1) == lane - R, selt_ref[:, e1:e1 + 1], 0.0)
        tiles.append(jnp.where(lane < R, w0, w1).astype(BF16))
    w = jnp.concatenate(tiles, axis=1)

    for cp in fetch(t, slot):
        cp.wait()
    o_ref[...] = x_ref[...] + _dot(w, ybuf[slot])

    for e in range(N_EXPERTS):
        def extra(s, carry):
            obuf[R:2 * R, :] = jnp.zeros((R, D_MODEL), obuf.dtype)
            row0 = pl.multiple_of(off_ref[e, t] + s * R, ROW_ALIGN)
            cp = pltpu.make_async_copy(y_hbm.at[pl.ds(row0, R), :], obuf.at[pl.ds(0, R), :], sem_o)
            cp.start()
            cp.wait()
            ws = jnp.where(local_pos(e) - jnp.asarray(s * R, F32) == lane, selt_ref[:, e:e + 1], 0.0)
            o_ref[...] += _dot(ws.astype(BF16), obuf[...])
            return carry
        lax.fori_loop(1, nsub_ref[e, t], extra, 0)


def moe_layer(x2, aff, groups, g_ffn, w_gate, w_up, w_down):
    n, D = x2.shape
    E = N_EXPERTS
    ff = w_gate.shape[-1]
    nt_total = n // MOE_TILE
    sels, poss, offs, nsubs, useds, blks, valids, flush_after = [], [], [], [], [], [], [], []
    tok0, tile0, row_base = 0, 0, 0
    for g, (ng, cap) in enumerate(groups):
        sel, pos, carry = route_select(aff[:, tok0:tok0 + ng], cap)
        sels.append(sel)
        poss.append(pos)
        ntile = ng // MOE_TILE
        cpad = _round_up(cap + ROW_ALIGN * ntile + MOE_STRIP, MOE_BLOCK) + MOE_BLOCK
        start = carry[:, ::MOE_TILE // LANES]
        cnt = jnp.diff(start, axis=1, append=jnp.full((E, 1), cap, jnp.int32))
        padded = _round_up(cnt, ROW_ALIGN)
        region = row_base + cpad * jnp.arange(E, dtype=jnp.int32)[:, None]
        off = jnp.cumsum(padded, axis=1) - padded
        used = jnp.sum(padded, axis=1, keepdims=True)
        offs.append(region + off)
        nsubs.append(_round_up(cnt, MOE_STRIP) // MOE_STRIP)
        useds.append(region + used)
        nblk = cpad // MOE_BLOCK
        last = (used + MOE_STRIP - 1) // MOE_BLOCK
        c = jnp.arange(nblk, dtype=jnp.int32)[None, :]
        blks.append(region // MOE_BLOCK + c)
        valids.append((c <= last).astype(jnp.int32))
        flush_after.append((tile0 + ntile - 1, g, row_base, cpad))
        tok0 += ng
        tile0 += ntile
        row_base += E * cpad
    sel = jnp.concatenate(sels, axis=1)
    pos = jnp.concatenate(poss, axis=1)
    off = jnp.concatenate(offs, axis=1).astype(jnp.int32)
    nsub = jnp.concatenate(nsubs, axis=1).astype(jnp.int32)
    used = jnp.concatenate(useds, axis=1).astype(jnp.int32)
    blk = jnp.concatenate(blks, axis=1).astype(jnp.int32)
    valid = jnp.concatenate(valids, axis=1)
    rows_total = row_base
    any_spec = pl.BlockSpec(memory_space=pl.ANY)
    tile_rows = pl.BlockSpec((E, 1, 1, MOE_TILE), lambda t, *_: (0, t, 0, 0))
    by_tile = lambda a: a.reshape(E, nt_total, 1, MOE_TILE)

    xe, ge = pl.pallas_call(
        functools.partial(_moe_dispatch_kernel, tuple(flush_after)),
        grid_spec=pltpu.PrefetchScalarGridSpec(
            num_scalar_prefetch=3, grid=(nt_total,),
            in_specs=[pl.BlockSpec((MOE_TILE, D), lambda t, *_: (t, 0)),
                      pl.BlockSpec((1, D), lambda t, *_: (0, 0)),
                      tile_rows, tile_rows, tile_rows],
            out_specs=[any_spec, any_spec],
            scratch_shapes=[pltpu.VMEM((2, E * MOE_STRIP, D), BF16), pltpu.VMEM((2, E * MOE_STRIP, LANES), F32),
                            pltpu.VMEM((MOE_TILE, D), BF16),
                            pltpu.VMEM((MOE_STRIP, D), BF16), pltpu.VMEM((MOE_STRIP, LANES), F32),
                            pltpu.VMEM((MOE_BLOCK, D), BF16), pltpu.VMEM((MOE_BLOCK, LANES), F32),
                            pltpu.SemaphoreType.DMA((2,)), pltpu.SemaphoreType.DMA((2,)),
                            pltpu.SemaphoreType.DMA((2,))]),
        out_shape=[jax.ShapeDtypeStruct((rows_total, D), BF16), jax.ShapeDtypeStruct((rows_total, LANES), F32)],
        compiler_params=_cparams(("arbitrary",)),
        name="moe_dispatch",
    )(off, nsub, used, x2, g_ffn.reshape(1, D), by_tile(sel), by_tile(pos), by_tile(aff))

    nblk_total = blk.shape[1]
    y = pl.pallas_call(
        _moe_ffn_kernel,
        grid_spec=pltpu.PrefetchScalarGridSpec(
            num_scalar_prefetch=2, grid=(E, nblk_total),
            in_specs=[pl.BlockSpec((MOE_BLOCK, D), lambda e, c, blk, valid: (blk[e, c], 0)),
                      pl.BlockSpec((MOE_BLOCK, LANES), lambda e, c, blk, valid: (blk[e, c], 0)),
                      pl.BlockSpec((1, D, ff), lambda e, c, *_: (e, 0, 0)),
                      pl.BlockSpec((1, D, ff), lambda e, c, *_: (e, 0, 0)),
                      pl.BlockSpec((1, ff, D), lambda e, c, *_: (e, 0, 0))],
            out_specs=pl.BlockSpec((MOE_BLOCK, D), lambda e, c, blk, valid: (blk[e, c], 0))),
        out_shape=jax.ShapeDtypeStruct((rows_total, D), BF16),
        compiler_params=_cparams(("arbitrary", "arbitrary")),
        name="moe_ffn",
    )(blk, valid, xe, ge, w_gate, w_up, w_down)

    sel_t = sel.reshape(E, n).T
    pos_t = pos.reshape(E, n).T
    return pl.pallas_call(
        _moe_combine_kernel,
        grid_spec=pltpu.PrefetchScalarGridSpec(
            num_scalar_prefetch=2, grid=(nt_total,),
            in_specs=[pl.BlockSpec((MOE_TILE, D), lambda t, *_: (t, 0)),
                      pl.BlockSpec((MOE_TILE, E), lambda t, *_: (t, 0)),
                      pl.BlockSpec((MOE_TILE, E), lambda t, *_: (t, 0)),
                      any_spec],
            out_specs=pl.BlockSpec((MOE_TILE, D), lambda t, *_: (t, 0)),
            scratch_shapes=[pltpu.VMEM((2, E * MOE_STRIP, D), BF16), pltpu.VMEM((2 * MOE_STRIP, D), BF16),
                            pltpu.SemaphoreType.DMA((2,)), pltpu.SemaphoreType.DMA]),
        out_shape=jax.ShapeDtypeStruct((n, D), F32),
        compiler_params=_cparams(("arbitrary",)),
        name="moe_combine",
    )(off, nsub, x2, sel_t, pos_t, y)


def kernel(x_prompt, x_sample, mem_prompt, mem_sample, norm_mix_g, w_in, mlstm_gate_b, mlstm_head_g, mla_qa_g, mla_w_uq, mla_kva_g, mla_w_ukv, mla_q_g, mla_k_g, rwkv_mu, rwkv_w0, rwkv_w_w2, rwkv_a0, rwkv_w_a2, rwkv_w_g2, rwkv_k_k, rwkv_k_a, rwkv_r_k, rwkv_ln_g, rwkv_ln_b, lru_conv_w, lru_conv_b, lru_gate_w, lru_gate_b, lru_lambda, w_out, norm_xa_g, norm_mem_g, xa_wq, xa_wkv, xa_q_g, xa_k_g, xa_wo, norm_ffn_g, moe_router, moe_w_gate, moe_w_up, moe_w_down):
    assert x_prompt.shape[1] == x_sample.shape[1]
    n_prompt = x_prompt.shape[0]
    x = jnp.concatenate([x_prompt, x_sample], axis=0)
    mem = jnp.concatenate([mem_prompt, mem_sample], axis=0)
    S, T, D = x.shape
    N = S * T
    tok_prompt = n_prompt * T
    tok_sample = N - tok_prompt
    cap_prompt = max(1, (CAPACITY_FACTOR * tok_prompt) // N_EXPERTS)
    cap_sample = max(1, (CAPACITY_FACTOR * tok_sample) // N_EXPERTS)
    w_in_p = prep_w_in(w_in)
    w_out_b = w_out.astype(BF16)
    w_gate_b = moe_w_gate.astype(BF16)
    w_up_b = moe_w_up.astype(BF16)
    w_down_b = moe_w_down.astype(BF16)
    x = x.reshape(N, D)
    for l in range(w_in.shape[0]):
        pm, pa, pr, pl_ = in_proj(x, norm_mix_g[l].reshape(1, D), w_in_p[l])
        y_m = mlstm_mixer(pm.reshape(S, T, -1), mlstm_gate_b[l], mlstm_head_g[l])
        y_a = mla_mixer(pa.reshape(S, T, -1), mla_qa_g[l], mla_w_uq[l], mla_kva_g[l], mla_w_ukv[l], mla_q_g[l],
                        mla_k_g[l])
        y_r = rwkv_mixer(pr.reshape(S, T, -1), rwkv_mu[l], rwkv_w0[l], rwkv_w_w2[l], rwkv_a0[l], rwkv_w_a2[l],
                         rwkv_w_g2[l], rwkv_k_k[l], rwkv_k_a[l], rwkv_r_k[l], rwkv_ln_g[l], rwkv_ln_b[l])
        y_l = lru_mixer(pl_.reshape(S, T, -1), lru_conv_w[l], lru_conv_b[l], lru_gate_w[l], lru_gate_b[l],
                        lru_lambda[l])
        x1 = out_proj(x, [y.reshape(N, GROUP_W) for y in (y_m, y_a, y_r, y_l)], w_out_b[l])
        k_mem, v_mem = mem_kv(mem, norm_mem_g[l], xa_wkv[l], xa_k_g[l])
        x2, aff = xattn_router(x1.reshape(S, T, D), norm_xa_g[l], xa_wq[l], xa_q_g[l], k_mem, v_mem,
                               xa_wo[l], norm_ffn_g[l], moe_router[l])
        x = moe_layer(x2.reshape(N, D), aff, [(tok_prompt, cap_prompt), (tok_sample, cap_sample)],
                      norm_ffn_g[l], w_gate_b[l], w_up_b[l], w_down_b[l])
    x = x.reshape(S, T, D)
    return x[:n_prompt], x[n_prompt:]
```
